```python
import jax
import jax.numpy as jnp
from jax import lax
import numpy as np

D_MODEL = 1024
BATCH = 16
SEQ = 2048
DEPTH = 4

N_MIXERS = 3
N_MOBA_LAYERS = (DEPTH + 2) // 3
N_HGRN_LAYERS = (DEPTH + 1) // 3
N_RGLRU_LAYERS = DEPTH // 3

MOBA_HEADS = 8
MOBA_HEAD_DIM = D_MODEL // MOBA_HEADS
MOBA_BLOCK = 256
MOBA_TOPK = 3
MOBA_Q_CHUNK = 64

HGRN_HEADS = 8
HGRN_KEY_DIM = 128
HGRN_FORGET_DIM = HGRN_HEADS * HGRN_KEY_DIM
HGRN_VAL_DIM = D_MODEL // HGRN_HEADS
HGRN_CHUNK = 64

RG_WIDTH = D_MODEL
RG_BLOCKS = 4
RG_BLOCK_WIDTH = RG_WIDTH // RG_BLOCKS
RG_CONV_WIDTH = 4
RG_C = 8.0

D_FF = 4 * D_MODEL
NORM_EPS = 1e-6

kernel_name = 'hybrid_moba_hgrn2_rglru_adaln'


def rms_norm(x, gain):
    xf = x.astype(jnp.float32)
    y = xf * lax.rsqrt(jnp.mean(xf * xf, axis=-1, keepdims=True) + NORM_EPS)
    return (y * gain.astype(jnp.float32)).astype(x.dtype)


def modulate(h, shift, scale):
    return h * (1.0 + scale[:, None, :]) + shift[:, None, :]


def sq_relu_mlp(h, w_up, w_down):
    u = jax.nn.relu(h @ w_up)
    return (u * u) @ w_down


def moba_attention(h, w_qkv, w_o):
    bsz, seq, _ = h.shape
    H, Dh, BLK, QC = MOBA_HEADS, MOBA_HEAD_DIM, MOBA_BLOCK, MOBA_Q_CHUNK
    q, k, v = jnp.split(h @ w_qkv, 3, axis=-1)
    n_blk = -(-seq // BLK)
    s_pad = n_blk * BLK

    def heads_padded(t):
        t = t.reshape(bsz, seq, H, Dh).transpose(0, 2, 1, 3)
        return jnp.pad(t, ((0, 0), (0, 0), (0, s_pad - seq), (0, 0)))

    q, k, v = heads_padded(q), heads_padded(k), heads_padded(v)
    kb = k.reshape(bsz, H, n_blk, BLK, Dh)
    vb = v.reshape(bsz, H, n_blk, BLK, Dh)
    n_sel = min(MOBA_TOPK, n_blk - 1)
    n_chunks = s_pad // QC
    q_chunks = q.reshape(bsz, H, n_chunks, QC, Dh).transpose(2, 0, 1, 3, 4)
    chunk_ids = jnp.arange(n_chunks)
    scale = Dh ** -0.5

    if n_sel > 0:
        q_blk = jnp.arange(s_pad) // BLK
        k_mean = jnp.mean(kb.astype(jnp.float32), axis=3)
        gate = jnp.einsum('bhsd,bhnd->bhsn', q.astype(jnp.float32), k_mean)
        fully_past = jnp.arange(n_blk)[None, :] < q_blk[:, None]
        gate = jnp.where(fully_past, gate, -jnp.inf)
        _, sel = lax.top_k(gate, n_sel)
        sel_chunks = sel.reshape(bsz, H, n_chunks, QC, n_sel).transpose(2, 0, 1, 3, 4)
        xs = (chunk_ids, q_chunks, sel_chunks)
    else:
        xs = (chunk_ids, q_chunks)

    b_idx = jnp.arange(bsz)[:, None, None, None]
    h_idx = jnp.arange(H)[None, :, None, None]

    def attend_chunk(args):
        ci, q_c = args[0], args[1]
        own = (ci * QC) // BLK
        q_pos = ci * QC + jnp.arange(QC)
        k_pos = own * BLK + jnp.arange(BLK)
        k_own = lax.dynamic_index_in_dim(kb, own, axis=2, keepdims=False)
        v_own = lax.dynamic_index_in_dim(vb, own, axis=2, keepdims=False)
        l_own = jnp.einsum('bhqd,bhkd->bhqk', q_c, k_own).astype(jnp.float32) * scale
        l_own = jnp.where(k_pos[None, :] <= q_pos[:, None], l_own, -jnp.inf)
        if n_sel == 0:
            p = jax.nn.softmax(l_own, axis=-1).astype(v_own.dtype)
            return jnp.einsum('bhqk,bhkd->bhqd', p, v_own)
        sel_c = args[2]
        k_sel = kb[b_idx, h_idx, sel_c]
        v_sel = vb[b_idx, h_idx, sel_c]
        l_sel = jnp.einsum('bhqd,bhqnkd->bhqnk', q_c, k_sel).astype(jnp.float32) * scale
        l_sel = jnp.where((sel_c < own)[..., None], l_sel, -jnp.inf)
        logits = jnp.concatenate([l_sel.reshape(bsz, H, QC, n_sel * BLK), l_own], axis=-1)
        p = jax.nn.softmax(logits, axis=-1).astype(v_own.dtype)
        p_sel = p[..., :n_sel * BLK].reshape(bsz, H, QC, n_sel, BLK)
        return (jnp.einsum('bhqnk,bhqnkd->bhqd', p_sel, v_sel)
                + jnp.einsum('bhqk,bhkd->bhqd', p[..., n_sel * BLK:], v_own))

    out = lax.map(attend_chunk, xs)
    out = out.transpose(1, 0, 3, 2, 4).reshape(bsz, s_pad, H * Dh)[:, :seq]
    return out @ w_o


def hgrn2_mixer(h, w_in, lb, g_gain, w_o):
    bsz, seq, _ = h.shape
    H, K, V, C = HGRN_HEADS, HGRN_KEY_DIM, HGRN_VAL_DIM, HGRN_CHUNK
    F = HGRN_FORGET_DIM
    q, f, i, g = jnp.split(h @ w_in, [F, 2 * F, 2 * F + H * V], axis=-1)
    q = jax.nn.silu(q.astype(jnp.float32))
    fgate = lb + (1.0 - lb) * jax.nn.sigmoid(f.astype(jnp.float32))
    k = 1.0 - fgate
    log_f = jnp.log(fgate)
    nc = seq // C

    def chunks(t, d):
        return t.reshape(bsz, nc, C, H, d).transpose(0, 3, 1, 2, 4)

    q, k, log_f = chunks(q, K), chunks(k, K), chunks(log_f, K)
    v = chunks(i.astype(jnp.float32), V)
    b = jnp.cumsum(log_f, axis=3)
    b_ref = b[:, :, :, C // 2:C // 2 + 1, :]
    b_last = b[:, :, :, C - 1:C, :]
    a = jnp.einsum('bhnck,bhnsk->bhncs', q * jnp.exp(b - b_ref), k * jnp.exp(b_ref - b))
    causal = jnp.tril(jnp.ones((C, C), dtype=bool))
    a = jnp.where(causal, a, 0.0)
    o_intra = jnp.einsum('bhncs,bhnsv->bhncv', a, v)
    q_in = q * jnp.exp(b)
    k_out = k * jnp.exp(b_last - b)
    decay = jnp.exp(b_last[:, :, :, 0, :])

    def step(state, xs):
        q_c, k_c, v_c, d_c = xs
        o = jnp.einsum('bhck,bhkv->bhcv', q_c, state)
        state = d_c[..., None] * state + jnp.einsum('bhck,bhcv->bhkv', k_c, v_c)
        return state, o

    xs = tuple(jnp.moveaxis(t, 2, 0) for t in (q_in, k_out, v, decay))
    state0 = jnp.zeros((bsz, H, K, V), jnp.float32)
    _, o_inter = lax.scan(step, state0, xs)
    o = o_intra + jnp.moveaxis(o_inter, 0, 2)
    o = o.transpose(0, 2, 3, 1, 4).reshape(bsz, seq, H, V)
    o = rms_norm(o, g_gain).reshape(bsz, seq, H * V) * jax.nn.silu(g.astype(jnp.float32))
    return o.astype(h.dtype) @ w_o


def rglru_mixer(h, w_in, conv_w, conv_b, w_a, b_a, w_i, b_i, lam, w_o):
    bsz, seq, _ = h.shape
    y_br, x_br = jnp.split(h @ w_in, 2, axis=-1)
    y_br = jax.nn.gelu(y_br)
    xp = jnp.pad(x_br, ((0, 0), (RG_CONV_WIDTH - 1, 0), (0, 0)))
    x_conv = conv_b
    for j in range(RG_CONV_WIDTH):
        x_conv = x_conv + xp[:, j:j + seq, :] * conv_w[j]
    xb = x_conv.reshape(bsz, seq, RG_BLOCKS, RG_BLOCK_WIDTH)
    r = jax.nn.sigmoid(jnp.einsum('bsnd,nde->bsne', xb, w_a).reshape(bsz, seq, RG_WIDTH) + b_a)
    gi = jax.nn.sigmoid(jnp.einsum('bsnd,nde->bsne', xb, w_i).reshape(bsz, seq, RG_WIDTH) + b_i)
    log_a = -RG_C * r.astype(jnp.float32) * jax.nn.softplus(-lam.astype(jnp.float32))
    a = jnp.exp(log_a)
    mult = jnp.sqrt(-jnp.expm1(2.0 * log_a))
    first = (jnp.arange(seq) == 0)[None, :, None]
    mult = jnp.where(first, 1.0, mult)
    u = (gi * x_conv).astype(jnp.float32) * mult

    def combine(left, right):
        return left[0] * right[0], right[0] * left[1] + right[1]

    _, hs = lax.associative_scan(combine, (a, u), axis=1)
    return (hs.astype(h.dtype) * y_br) @ w_o


def setup_inputs(seed: int = 0) -> dict:
    key = jax.random.key(seed)
    ks = jax.random.split(key, 24)
    f32 = jnp.float32
    D = D_MODEL

    def nrm(k, shape, scale):
        return jax.random.normal(k, shape, f32) * scale

    a_target = jax.random.uniform(ks[21], (N_RGLRU_LAYERS, RG_WIDTH), f32, 0.9, 0.999)
    s_root = a_target ** (1.0 / RG_C)
    return {
        'x': nrm(ks[0], (BATCH, SEQ, D), 1.0),
        'c': nrm(ks[1], (BATCH, D), 1.0),
        'ada_w': nrm(ks[2], (DEPTH, D, 6 * D), 0.5 * D ** -0.5),
        'ada_b': nrm(ks[3], (DEPTH, 6 * D), 0.02),
        'norm_mix': 1.0 + nrm(ks[4], (DEPTH, D), 0.02),
        'norm_mlp': 1.0 + nrm(ks[5], (DEPTH, D), 0.02),
        'mlp_up': nrm(ks[6], (DEPTH, D, D_FF), D ** -0.5),
        'mlp_down': nrm(ks[7], (DEPTH, D_FF, D), D_FF ** -0.5),
        'moba_wqkv': nrm(ks[8], (N_MOBA_LAYERS, D, 3 * MOBA_HEADS * MOBA_HEAD_DIM), D ** -0.5),
        'moba_wo': nrm(ks[9], (N_MOBA_LAYERS, MOBA_HEADS * MOBA_HEAD_DIM, D), D ** -0.5),
        'hgrn_w_in': nrm(ks[10], (N_HGRN_LAYERS, D, 2 * HGRN_FORGET_DIM + 2 * HGRN_HEADS * HGRN_VAL_DIM), D ** -0.5),
        'hgrn_lb': nrm(ks[11], (DEPTH, HGRN_FORGET_DIM), 1.0),
        'hgrn_norm': 1.0 + nrm(ks[12], (N_HGRN_LAYERS, HGRN_VAL_DIM), 0.02),
        'hgrn_wo': nrm(ks[13], (N_HGRN_LAYERS, HGRN_HEADS * HGRN_VAL_DIM, D), D ** -0.5),
        'rg_w_in': nrm(ks[14], (N_RGLRU_LAYERS, D, 2 * RG_WIDTH), D ** -0.5),
        'rg_conv_w': nrm(ks[15], (N_RGLRU_LAYERS, RG_CONV_WIDTH, RG_WIDTH), RG_CONV_WIDTH ** -0.5),
        'rg_conv_b': nrm(ks[16], (N_RGLRU_LAYERS, RG_WIDTH), 0.02),
        'rg_w_a': nrm(ks[17], (N_RGLRU_LAYERS, RG_BLOCKS, RG_BLOCK_WIDTH, RG_BLOCK_WIDTH), RG_BLOCK_WIDTH ** -0.5),
        'rg_b_a': nrm(ks[18], (N_RGLRU_LAYERS, RG_WIDTH), 0.02),
        'rg_w_i': nrm(ks[19], (N_RGLRU_LAYERS, RG_BLOCKS, RG_BLOCK_WIDTH, RG_BLOCK_WIDTH), RG_BLOCK_WIDTH ** -0.5),
        'rg_b_i': nrm(ks[20], (N_RGLRU_LAYERS, RG_WIDTH), 0.02),
        'rg_lambda': jnp.log(s_root) - jnp.log1p(-s_root),
        'rg_wo': nrm(ks[22], (N_RGLRU_LAYERS, RG_WIDTH, D), RG_WIDTH ** -0.5),
        'final_norm': 1.0 + nrm(ks[23], (D,), 0.02),
    }


def reference(x, c, ada_w, ada_b, norm_mix, norm_mlp, mlp_up, mlp_down,
              moba_wqkv, moba_wo, hgrn_w_in, hgrn_lb, hgrn_norm, hgrn_wo,
              rg_w_in, rg_conv_w, rg_conv_b, rg_w_a, rg_b_a, rg_w_i, rg_b_i,
              rg_lambda, rg_wo, final_norm):
    cond = jax.nn.silu(c)
    lb_all = jnp.cumsum(jax.nn.softmax(hgrn_lb.astype(jnp.float32), axis=0), axis=0)
    lb_all = lb_all - lb_all[0:1]
    i_a = 0
    i_b = 0
    i_c = 0
    for layer in range(DEPTH):
        mod = cond @ ada_w[layer] + ada_b[layer]
        shift1, scale1, gate1, shift2, scale2, gate2 = jnp.split(mod, 6, axis=-1)
        h = modulate(rms_norm(x, norm_mix[layer]), shift1, scale1)
        kind = layer % N_MIXERS
        if kind == 0:
            y = moba_attention(h, moba_wqkv[i_a], moba_wo[i_a])
            i_a += 1
        elif kind == 1:
            y = hgrn2_mixer(h, hgrn_w_in[i_b], lb_all[layer], hgrn_norm[i_b], hgrn_wo[i_b])
            i_b += 1
        else:
            y = rglru_mixer(h, rg_w_in[i_c], rg_conv_w[i_c], rg_conv_b[i_c], rg_w_a[i_c],
                            rg_b_a[i_c], rg_w_i[i_c], rg_b_i[i_c], rg_lambda[i_c], rg_wo[i_c])
            i_c += 1
        x = x + gate1[:, None, :] * y
        h = modulate(rms_norm(x, norm_mlp[layer]), shift2, scale2)
        x = x + gate2[:, None, :] * sq_relu_mlp(h, mlp_up[layer], mlp_down[layer])
    return rms_norm(x, final_norm)
```

```python
import functools

import jax
import jax.numpy as jnp
from jax import lax
from jax.experimental import pallas as pl
from jax.experimental.pallas import tpu as pltpu

_F32 = jnp.float32
_BF16 = jnp.bfloat16
_HIGHEST = lax.Precision.HIGHEST

NORM_EPS = 1e-6
N_MIXERS = 3
MOBA_HEADS = 8
MOBA_BLOCK = 256
MOBA_TOPK = 3
HGRN_HEADS = 8
HGRN_CHUNK = 64
RG_BLOCKS = 4
RG_CONV_WIDTH = 4
RG_C = 8.0
MASK_VALUE = -1e30

V7X_VMEM_LIMIT_BYTES = 56 * 1024 * 1024
LANES = 128


def _tiles(seq):
    return dict(proj=min(seq, 256), tail=min(seq, 512), hgrn=min(seq, 256), rglru=min(seq, 256))


def _cparams(*sem):
    return pltpu.CompilerParams(dimension_semantics=sem, vmem_limit_bytes=V7X_VMEM_LIMIT_BYTES)


def _resident(shape):
    zeros = (0,) * len(shape)
    return pl.BlockSpec(shape, lambda *_: zeros, pipeline_mode=pl.Buffered(1))


def _sigmoid(x):
    return 1.0 / (1.0 + jnp.exp(-x))


def _rms_norm(x, gain):
    return x * lax.rsqrt(jnp.mean(x * x, axis=-1, keepdims=True) + NORM_EPS) * gain


def _norm_mod(x, gain, shift, scale):
    return _rms_norm(x, gain) * (1.0 + scale) + shift


def _dot(a, b):
    return jnp.dot(a, b, preferred_element_type=_F32)


def _dot_nt(a, b):
    return lax.dot_general(a, b, (((1,), (1,)), ((), ())), preferred_element_type=_F32)


def _dot_tn(a, b):
    return lax.dot_general(a, b, (((0,), (0,)), ((), ())), preferred_element_type=_F32)


def _ada_kernel(c_ref, w_ref, b_ref, o_ref):
    c = c_ref[...]
    cond = c * _sigmoid(c)
    o_ref[0] = jnp.dot(cond, w_ref[0], preferred_element_type=_F32, precision=_HIGHEST) + b_ref[0]


def _ada_modulation(c, ada_w, ada_b):
    depth, d, n = ada_w.shape
    bsz = c.shape[0]
    tn = min(n, 1536)
    return pl.pallas_call(
        _ada_kernel,
        grid=(depth, n // tn),
        in_specs=[
            pl.BlockSpec((bsz, d), lambda l, j: (0, 0)),
            pl.BlockSpec((1, d, tn), lambda l, j: (l, 0, j)),
            pl.BlockSpec((1, 1, tn), lambda l, j: (l, 0, j)),
        ],
        out_specs=pl.BlockSpec((1, bsz, tn), lambda l, j: (l, 0, j)),
        out_shape=jax.ShapeDtypeStruct((depth, bsz, n), _F32),
        compiler_params=_cparams("arbitrary", "arbitrary"),
        name="ada_modulation",
    )(c, ada_w, ada_b.reshape(depth, 1, n))


def _lb_kernel(x_ref, o_ref):
    depth = x_ref.shape[0]
    rows = [x_ref[l:l + 1, :] for l in range(depth)]
    mx = functools.reduce(jnp.maximum, rows)
    es = [jnp.exp(r - mx) for r in rows]
    den = functools.reduce(lambda a, b: a + b, es)
    first = es[0] / den
    run = first
    o_ref[0:1, :] = run - first
    for l in range(1, depth):
        run = run + es[l] / den
        o_ref[l:l + 1, :] = run - first


def _hgrn_lower_bounds(hgrn_lb):
    return pl.pallas_call(
        _lb_kernel,
        out_shape=jax.ShapeDtypeStruct(hgrn_lb.shape, _F32),
        name="hgrn_lower_bounds",
    )(hgrn_lb.astype(_F32))


def _proj_kernel(x_ref, g_ref, sh_ref, sc_ref, w_ref, o_ref, *, n_chunk):
    h = _norm_mod(x_ref[0], g_ref[...], sh_ref[0], sc_ref[0]).astype(_BF16)
    for c in range(0, w_ref.shape[1], n_chunk):
        o_ref[0, :, c:c + n_chunk] = _dot(h, w_ref[:, c:c + n_chunk]).astype(o_ref.dtype)


def _project(x, gain, mod, w, out_dtype, tm):
    bsz, seq, d = x.shape
    n = w.shape[1]
    return pl.pallas_call(
        functools.partial(_proj_kernel, n_chunk=min(n, 1024)),
        grid=(bsz, seq // tm),
        in_specs=[
            pl.BlockSpec((1, tm, d), lambda b, t: (b, t, 0)),
            _resident((1, d)),
            pl.BlockSpec((1, 1, d), lambda b, t: (b, 0, 0)),
            pl.BlockSpec((1, 1, d), lambda b, t: (b, 0, 1)),
            _resident((d, n)),
        ],
        out_specs=pl.BlockSpec((1, tm, n), lambda b, t: (b, t, 0)),
        out_shape=jax.ShapeDtypeStruct((bsz, seq, n), out_dtype),
        compiler_params=_cparams("parallel", "parallel"),
        name="project",
    )(x, gain.reshape(1, d), mod, mod, w)


def _moba_kernel(q_ref, k_ref, v_ref, o_ref, m_sc, l_sc, acc_sc, *, blk, topk, scale):
    i = pl.program_id(2)
    seq = k_ref.shape[1]
    nblk = seq // blk
    q = q_ref[0]

    row = lax.broadcasted_iota(jnp.int32, (nblk, seq), 0)
    col = lax.broadcasted_iota(jnp.int32, (nblk, seq), 1)
    ind = jnp.logical_and(col >= row * blk, col < (row + 1) * blk).astype(_BF16)
    kmean = _dot(ind, k_ref[0]) * (1.0 / blk)
    kmean_hi = kmean.astype(_BF16)
    kmean_lo = (kmean - kmean_hi.astype(_F32)).astype(_BF16)
    gate_t = _dot_nt(kmean_hi, q) + _dot_nt(kmean_lo, q)

    blk_id = lax.broadcasted_iota(jnp.int32, (nblk, blk), 0)
    cand = blk_id < i
    sel_rows = []
    for j in range(nblk):
        g_j = gate_t[j:j + 1, :]
        beats = jnp.logical_and(
            cand, jnp.logical_or(gate_t > g_j, jnp.logical_and(gate_t == g_j, blk_id < j)))
        rank = jnp.sum(beats.astype(_F32), axis=0, keepdims=True)
        sel_rows.append(jnp.where(jnp.logical_and(rank < topk, j < i), 1.0, 0.0))
    sel_t = jnp.concatenate(sel_rows + [jnp.zeros((LANES - nblk, blk), _F32)], axis=0)
    eye = (lax.broadcasted_iota(jnp.int32, (blk, blk), 0)
           == lax.broadcasted_iota(jnp.int32, (blk, blk), 1)).astype(_BF16)
    sel = _dot_nt(eye, sel_t.astype(_BF16))

    start = pl.multiple_of(i * blk, blk)
    s = _dot_nt(q, k_ref[0, pl.ds(start, blk), :]) * scale
    qpos = lax.broadcasted_iota(jnp.int32, (blk, blk), 0)
    kpos = lax.broadcasted_iota(jnp.int32, (blk, blk), 1)
    s = jnp.where(kpos <= qpos, s, MASK_VALUE)
    m0 = jnp.max(s, axis=-1, keepdims=True)
    p = jnp.exp(s - m0)
    m_sc[...] = m0
    l_sc[...] = jnp.sum(p, axis=-1, keepdims=True)
    acc_sc[...] = _dot(p.astype(_BF16), v_ref[0, pl.ds(start, blk), :])

    for j in range(nblk - 1):
        @pl.when(j < i)
        def _past(j=j):
            sj = _dot_nt(q, k_ref[0, j * blk:(j + 1) * blk, :]) * scale
            sj = jnp.where(sel[:, j:j + 1] > 0.5, sj, MASK_VALUE)
            m_old = m_sc[...]
            m_new = jnp.maximum(m_old, jnp.max(sj, axis=-1, keepdims=True))
            alpha = jnp.exp(m_old - m_new)
            pj = jnp.exp(sj - m_new)
            m_sc[...] = m_new
            l_sc[...] = alpha * l_sc[...] + jnp.sum(pj, axis=-1, keepdims=True)
            acc_sc[...] = alpha * acc_sc[...] + _dot(pj.astype(_BF16), v_ref[0, j * blk:(j + 1) * blk, :])

    o_ref[0] = (acc_sc[...] * (1.0 / l_sc[...])).astype(o_ref.dtype)


def _moba_attention(qkv, heads):
    bsz, seq, three_hd = qkv.shape
    dh = three_hd // (3 * heads)
    blk = MOBA_BLOCK
    assert seq % blk == 0 and dh == LANES
    nblk = seq // blk
    topk = min(MOBA_TOPK, nblk - 1)
    kern = functools.partial(_moba_kernel, blk=blk, topk=topk, scale=dh ** -0.5)
    return pl.pallas_call(
        kern,
        grid=(bsz, heads, nblk),
        in_specs=[
            pl.BlockSpec((1, blk, dh), lambda b, h, i: (b, i, h)),
            pl.BlockSpec((1, seq, dh), lambda b, h, i: (b, 0, heads + h)),
            pl.BlockSpec((1, seq, dh), lambda b, h, i: (b, 0, 2 * heads + h)),
        ],
        out_specs=pl.BlockSpec((1, blk, dh), lambda b, h, i: (b, i, h)),
        out_shape=jax.ShapeDtypeStruct((bsz, seq, heads * dh), _BF16),
        scratch_shapes=[
            pltpu.VMEM((blk, 1), _F32),
            pltpu.VMEM((blk, 1), _F32),
            pltpu.VMEM((blk, dh), _F32),
        ],
        compiler_params=_cparams("parallel", "parallel", "arbitrary"),
        name="moba_attention",
    )(qkv, qkv, qkv)


def _hgrn_kernel(zq_ref, zf_ref, zi_ref, zg_ref, lb_ref, gain_ref, o_ref, st_ref, *, chunk):
    @pl.when(pl.program_id(2) == 0)
    def _init():
        st_ref[...] = jnp.zeros_like(st_ref)

    tm = zq_ref.shape[1]
    lb = lb_ref[...]
    gain = gain_ref[...]
    r_i = lax.broadcasted_iota(jnp.int32, (chunk, chunk), 0)
    c_i = lax.broadcasted_iota(jnp.int32, (chunk, chunk), 1)
    causal = c_i <= r_i
    tri = causal.astype(_F32)
    for c0 in range(0, tm, chunk):
        sl = slice(c0, c0 + chunk)
        zq = zq_ref[0, sl, :]
        q = zq * _sigmoid(zq)
        fgate = lb + (1.0 - lb) * _sigmoid(zf_ref[0, sl, :])
        k = 1.0 - fgate
        b = jnp.dot(tri, jnp.log(fgate), preferred_element_type=_F32, precision=_HIGHEST)
        b_mid = b[chunk // 2:chunk // 2 + 1, :]
        b_last = b[chunk - 1:chunk, :]
        v = zi_ref[0, sl, :].astype(_BF16)
        a = _dot_nt((q * jnp.exp(b - b_mid)).astype(_BF16), (k * jnp.exp(b_mid - b)).astype(_BF16))
        a = jnp.where(causal, a, 0.0)
        st = st_ref[...]
        o = _dot(a.astype(_BF16), v) + _dot_nt((q * jnp.exp(b)).astype(_BF16), st.astype(_BF16))
        k_out = (k * jnp.exp(b_last - b)).astype(_BF16)
        st_ref[...] = jnp.exp(b_last) * st + _dot_tn(v, k_out)
        zg = zg_ref[0, sl, :]
        o_ref[0, sl, :] = (_rms_norm(o, gain) * (zg * _sigmoid(zg))).astype(o_ref.dtype)


def _hgrn_mixer(z, lb, gain, heads, tm):
    bsz, seq, n = z.shape
    kdim = lb.shape[-1] // heads
    vdim = gain.shape[-1]
    assert kdim == LANES and vdim == LANES and n == 2 * heads * kdim + 2 * heads * vdim
    return pl.pallas_call(
        functools.partial(_hgrn_kernel, chunk=HGRN_CHUNK),
        grid=(bsz, heads, seq // tm),
        in_specs=[
            pl.BlockSpec((1, tm, kdim), lambda b, h, t: (b, t, h)),
            pl.BlockSpec((1, tm, kdim), lambda b, h, t: (b, t, heads + h)),
            pl.BlockSpec((1, tm, vdim), lambda b, h, t: (b, t, 2 * heads + h)),
            pl.BlockSpec((1, tm, vdim), lambda b, h, t: (b, t, 3 * heads + h)),
            pl.BlockSpec((1, kdim), lambda b, h, t: (0, h)),
            pl.BlockSpec((1, vdim), lambda b, h, t: (0, 0)),
        ],
        out_specs=pl.BlockSpec((1, tm, vdim), lambda b, h, t: (b, t, h)),
        out_shape=jax.ShapeDtypeStruct((bsz, seq, heads * vdim), _BF16),
        scratch_shapes=[pltpu.VMEM((vdim, kdim), _F32)],
        compiler_params=_cparams("parallel", "parallel", "arbitrary"),
        name="hgrn2_mixer",
    )(z, z, z, z, lb.reshape(1, -1), gain.reshape(1, -1))


def _gelu_tanh(x):
    return x * (0.5 * (1.0 + jnp.tanh(0.7978845608028654 * (x + 0.044715 * (x * x * x)))))


def _rglru_kernel(z_ref, cw_ref, cb_ref, wa_ref, ba_ref, wi_ref, bi_ref, lam_ref, o_ref,
                  hc_ref, xh_ref):
    t = pl.program_id(1)

    @pl.when(t == 0)
    def _init():
        hc_ref[...] = jnp.zeros_like(hc_ref)
        xh_ref[...] = jnp.zeros_like(xh_ref)

    tm = z_ref.shape[1]
    width = o_ref.shape[2]
    hist = xh_ref.shape[0]
    y_br = _gelu_tanh(z_ref[0, :, :width])
    x_br = z_ref[0, :, width:]
    xx = jnp.concatenate([xh_ref[...], x_br], axis=0)
    xh_ref[...] = x_br[tm - hist:, :]
    x_conv = cb_ref[...]
    for j in range(RG_CONV_WIDTH):
        off = hist - (RG_CONV_WIDTH - 1) + j
        x_conv = x_conv + xx[off:off + tm, :] * cw_ref[j:j + 1, :]

    nb = wa_ref.shape[0]
    bw = width // nb
    xcb = x_conv.astype(_BF16)
    r = jnp.concatenate([_dot(xcb[:, n * bw:(n + 1) * bw], wa_ref[n]) for n in range(nb)], axis=1)
    gi = jnp.concatenate([_dot(xcb[:, n * bw:(n + 1) * bw], wi_ref[n]) for n in range(nb)], axis=1)
    r = _sigmoid(r + ba_ref[...])
    gi = _sigmoid(gi + bi_ref[...])
    neg_lam = -lam_ref[...]
    softplus = jnp.maximum(neg_lam, 0.0) + jnp.log1p(jnp.exp(-jnp.abs(neg_lam)))
    log_a = -RG_C * r * softplus
    a = jnp.exp(log_a)
    th = jnp.tanh(log_a)
    mult = jnp.sqrt(-2.0 * th / (1.0 - th))
    row = lax.broadcasted_iota(jnp.int32, (tm, 1), 0)
    mult = jnp.where(jnp.logical_and(row == 0, t == 0), 1.0, mult)
    u = (gi * x_conv) * mult

    d = 1
    while d < tm:
        keep = row >= d
        a_sh = jnp.where(keep, pltpu.roll(a, d, 0), 1.0)
        u_sh = jnp.where(keep, pltpu.roll(u, d, 0), 0.0)
        u = u + a * u_sh
        a = a * a_sh
        d *= 2
    hs = u + a * hc_ref[...]
    hc_ref[...] = hs[tm - 1:tm, :]
    o_ref[0] = (hs * y_br).astype(o_ref.dtype)


def _rglru_mixer(z, conv_w, conv_b, w_a, b_a, w_i, b_i, lam, tm):
    bsz, seq, two_w = z.shape
    width = two_w // 2
    nb, bw, _ = w_a.shape
    row = lambda v: v.reshape(1, width)
    return pl.pallas_call(
        _rglru_kernel,
        grid=(bsz, seq // tm),
        in_specs=[
            pl.BlockSpec((1, tm, two_w), lambda b, t: (b, t, 0)),
            _resident((RG_CONV_WIDTH, width)),
            _resident((1, width)),
            _resident((nb, bw, bw)),
            _resident((1, width)),
            _resident((nb, bw, bw)),
            _resident((1, width)),
            _resident((1, width)),
        ],
        out_specs=pl.BlockSpec((1, tm, width), lambda b, t: (b, t, 0)),
        out_shape=jax.ShapeDtypeStruct((bsz, seq, width), _BF16),
        scratch_shapes=[pltpu.VMEM((1, width), _F32), pltpu.VMEM((8, width), _F32)],
        compiler_params=_cparams("parallel", "arbitrary"),
        name="rglru_mixer",
    )(z, conv_w, row(conv_b), w_a.astype(_BF16), row(b_a), w_i.astype(_BF16), row(b_i), row(lam))


def _tail_kernel(x_ref, m_ref, wo_ref, g1_ref, gn_ref, sh_ref, sc_ref, g2_ref, wup_ref, wdn_ref,
                 *rest, ff_chunk, final):
    o_ref = rest[-1]
    x1 = x_ref[0] + g1_ref[0] * _dot(m_ref[0], wo_ref[...])
    h = _norm_mod(x1, gn_ref[...], sh_ref[0], sc_ref[0]).astype(_BF16)
    acc = jnp.zeros_like(x1)
    for c in range(0, wup_ref.shape[1], ff_chunk):
        u = jnp.maximum(_dot(h, wup_ref[:, c:c + ff_chunk]), 0.0)
        acc = acc + _dot((u * u).astype(_BF16), wdn_ref[c:c + ff_chunk, :])
    x2 = x1 + g2_ref[0] * acc
    if final:
        x2 = _rms_norm(x2, rest[0][...])
    o_ref[0] = x2


def _layer_tail(x, m, w_o, mod, gain_mlp, w_up, w_down, final_gain, tm):
    bsz, seq, d = x.shape
    dm = m.shape[-1]
    dff = w_up.shape[1]
    final = final_gain is not None
    mod_spec = lambda idx: pl.BlockSpec((1, 1, d), lambda b, t: (b, 0, idx))
    in_specs = [
        pl.BlockSpec((1, tm, d), lambda b, t: (b, t, 0)),
        pl.BlockSpec((1, tm, dm), lambda b, t: (b, t, 0)),
        _resident((dm, d)),
        mod_spec(2),
        _resident((1, d)),
        mod_spec(3),
        mod_spec(4),
        mod_spec(5),
        _resident((d, dff)),
        _resident((dff, d)),
    ]
    args = [x, m, w_o, mod, gain_mlp.reshape(1, d), mod, mod, mod, w_up, w_down]
    if final:
        in_specs.append(_resident((1, d)))
        args.append(final_gain.reshape(1, d))
    return pl.pallas_call(
        functools.partial(_tail_kernel, ff_chunk=min(dff, 1024), final=final),
        grid=(bsz, seq // tm),
        in_specs=in_specs,
        out_specs=pl.BlockSpec((1, tm, d), lambda b, t: (b, t, 0)),
        out_shape=jax.ShapeDtypeStruct((bsz, seq, d), _F32),
        compiler_params=_cparams("parallel", "parallel"),
        name="layer_tail",
    )(*args)


def kernel(x, c, ada_w, ada_b, norm_mix, norm_mlp, mlp_up, mlp_down, moba_wqkv, moba_wo, hgrn_w_in, hgrn_lb, hgrn_norm, hgrn_wo, rg_w_in, rg_conv_w, rg_conv_b, rg_w_a, rg_b_a, rg_w_i, rg_b_i, rg_lambda, rg_wo, final_norm):
    depth = ada_w.shape[0]
    bsz, seq, d = x.shape
    tiles = _tiles(seq)
    bf = lambda w: w.astype(_BF16)

    mod_all = _ada_modulation(c, ada_w, ada_b)
    lb_all = _hgrn_lower_bounds(hgrn_lb)
    i_a = i_b = i_c = 0
    for layer in range(depth):
        mod = mod_all[layer].reshape(bsz, 1, 6 * d)
        kind = layer % N_MIXERS
        if kind == 0:
            qkv = _project(x, norm_mix[layer], mod, bf(moba_wqkv[i_a]), _BF16, tiles["proj"])
            m = _moba_attention(qkv, MOBA_HEADS)
            w_o = moba_wo[i_a]
            i_a += 1
        elif kind == 1:
            z = _project(x, norm_mix[layer], mod, bf(hgrn_w_in[i_b]), _F32, tiles["proj"])
            m = _hgrn_mixer(z, lb_all[layer], hgrn_norm[i_b], HGRN_HEADS, tiles["hgrn"])
            w_o = hgrn_wo[i_b]
            i_b += 1
        else:
            z = _project(x, norm_mix[layer], mod, bf(rg_w_in[i_c]), _F32, tiles["proj"])
            m = _rglru_mixer(z, rg_conv_w[i_c], rg_conv_b[i_c], rg_w_a[i_c], rg_b_a[i_c],
                             rg_w_i[i_c], rg_b_i[i_c], rg_lambda[i_c], tiles["rglru"])
            w_o = rg_wo[i_c]
            i_c += 1
        x = _layer_tail(x, m, bf(w_o), mod, norm_mlp[layer], bf(mlp_up[layer]), bf(mlp_down[layer]),
                        final_norm if layer == depth - 1 else None, tiles["tail"])
    return x
```

```python
import functools

import jax
import jax.numpy as jnp
from jax import lax
from jax.experimental import pallas as pl
from jax.experimental.pallas import tpu as pltpu

_F32 = jnp.float32
_BF16 = jnp.bfloat16
_HIGHEST = lax.Precision.HIGHEST

NORM_EPS = 1e-6
N_MIXERS = 3
MOBA_HEADS = 8
MOBA_BLOCK = 256
MOBA_TOPK = 3
HGRN_HEADS = 8
HGRN_CHUNK = 64
RG_BLOCKS = 4
RG_CONV_WIDTH = 4
RG_C = 8.0
MASK_VALUE = -1e30

V7X_VMEM_LIMIT_BYTES = 56 * 1024 * 1024
LANES = 128
SUBLANES = 8


def _tiles(seq):
    return dict(proj=min(seq, 256), tail=min(seq, 512), hgrn=min(seq, 256), rglru=min(seq, 256))


def _cparams(*sem):
    return pltpu.CompilerParams(dimension_semantics=sem, vmem_limit_bytes=V7X_VMEM_LIMIT_BYTES)


def _resident(shape):
    zeros = (0,) * len(shape)
    return pl.BlockSpec(shape, lambda *_: zeros, pipeline_mode=pl.Buffered(1))


def _sigmoid(x):
    return 1.0 / (1.0 + jnp.exp(-x))


def _sigmoid_tanh(x):
    return 0.5 * jnp.tanh(0.5 * x) + 0.5


def _rms_norm(x, gain):
    return x * lax.rsqrt(jnp.mean(x * x, axis=-1, keepdims=True) + NORM_EPS) * gain


def _norm_mod(x, gain, shift, scale):
    return _rms_norm(x, gain) * (1.0 + scale) + shift


def _dot(a, b):
    return jnp.dot(a, b, preferred_element_type=_F32)


def _dot_nt(a, b):
    return lax.dot_general(a, b, (((1,), (1,)), ((), ())), preferred_element_type=_F32)


def _dot_tn(a, b):
    return lax.dot_general(a, b, (((0,), (0,)), ((), ())), preferred_element_type=_F32)


def _ada_kernel(c_ref, w_ref, b_ref, o_ref):
    c = c_ref[...]
    cond = c * _sigmoid(c)
    o_ref[0] = jnp.dot(cond, w_ref[0], preferred_element_type=_F32, precision=_HIGHEST) + b_ref[0]


def _ada_modulation(c, ada_w, ada_b):
    depth, d, n = ada_w.shape
    bsz = c.shape[0]
    tn = min(n, 1536)
    return pl.pallas_call(
        _ada_kernel,
        grid=(depth, n // tn),
        in_specs=[
            pl.BlockSpec((bsz, d), lambda l, j: (0, 0)),
            pl.BlockSpec((1, d, tn), lambda l, j: (l, 0, j)),
            pl.BlockSpec((1, 1, tn), lambda l, j: (l, 0, j)),
        ],
        out_specs=pl.BlockSpec((1, bsz, tn), lambda l, j: (l, 0, j)),
        out_shape=jax.ShapeDtypeStruct((depth, bsz, n), _F32),
        compiler_params=_cparams("arbitrary", "arbitrary"),
        name="ada_modulation",
    )(c, ada_w, ada_b.reshape(depth, 1, n))


def _lb_kernel(x_ref, o_ref):
    depth = x_ref.shape[0]
    rows = [x_ref[l:l + 1, :] for l in range(depth)]
    mx = functools.reduce(jnp.maximum, rows)
    es = [jnp.exp(r - mx) for r in rows]
    den = functools.reduce(lambda a, b: a + b, es)
    first = es[0] / den
    run = first
    o_ref[0:1, :] = run - first
    for l in range(1, depth):
        run = run + es[l] / den
        o_ref[l:l + 1, :] = run - first


def _hgrn_lower_bounds(hgrn_lb):
    return pl.pallas_call(
        _lb_kernel,
        out_shape=jax.ShapeDtypeStruct(hgrn_lb.shape, _F32),
        name="hgrn_lower_bounds",
    )(hgrn_lb.astype(_F32))


def _proj_kernel(x_ref, g_ref, sh_ref, sc_ref, w_ref, o_ref, *, n_chunk):
    h = _norm_mod(x_ref[0], g_ref[...], sh_ref[0], sc_ref[0]).astype(_BF16)
    for c in range(0, w_ref.shape[1], n_chunk):
        o_ref[0, :, c:c + n_chunk] = _dot(h, w_ref[:, c:c + n_chunk]).astype(o_ref.dtype)


def _project(x, gain, mod, w, out_dtype, tm):
    bsz, seq, d = x.shape
    n = w.shape[1]
    return pl.pallas_call(
        functools.partial(_proj_kernel, n_chunk=min(n, 1024)),
        grid=(bsz, seq // tm),
        in_specs=[
            pl.BlockSpec((1, tm, d), lambda b, t: (b, t, 0)),
            _resident((1, d)),
            pl.BlockSpec((1, 1, d), lambda b, t: (b, 0, 0)),
            pl.BlockSpec((1, 1, d), lambda b, t: (b, 0, 1)),
            _resident((d, n)),
        ],
        out_specs=pl.BlockSpec((1, tm, n), lambda b, t: (b, t, 0)),
        out_shape=jax.ShapeDtypeStruct((bsz, seq, n), out_dtype),
        compiler_params=_cparams("parallel", "parallel"),
        name="project",
    )(x, gain.reshape(1, d), mod, mod, w)


def _moba_kernel(q_ref, k_ref, v_ref, o_ref, *, blk, topk, scale):
    seq, dh = k_ref.shape[1:]
    nblk = seq // blk
    q, k, v = q_ref[0], k_ref[0], v_ref[0]

    row = lax.broadcasted_iota(jnp.int32, (nblk, seq), 0)
    col = lax.broadcasted_iota(jnp.int32, (nblk, seq), 1)
    fully_past = (row + 1) * blk <= col
    in_block = jnp.logical_and(row * blk <= col, jnp.logical_not(fully_past))
    kmean = _dot(in_block.astype(_BF16), k) * (1.0 / blk)
    kmean_hi = kmean.astype(_BF16)
    kmean_lo = (kmean - kmean_hi.astype(_F32)).astype(_BF16)
    gate_t = _dot_nt(kmean_hi, q) + _dot_nt(kmean_lo, q)

    bias_rows = []
    for j in range(nblk):
        g_j = gate_t[j:j + 1, :]
        beats = jnp.logical_and(
            fully_past, jnp.logical_or(gate_t > g_j, jnp.logical_and(gate_t == g_j, row < j)))
        rank = jnp.sum(beats.astype(_F32), axis=0, keepdims=True)
        masked = jnp.logical_and(fully_past[j:j + 1, :], rank >= topk)
        bias_rows.append(jnp.where(masked, MASK_VALUE, 0.0))
    bias_t = jnp.concatenate(bias_rows + [jnp.zeros((dh - nblk, seq), _F32)], axis=0)

    key_pos = lax.broadcasted_iota(jnp.int32, (seq, dh), 0)
    lane = lax.broadcasted_iota(jnp.int32, (seq, dh), 1)
    onehot = jnp.logical_and(lane * blk <= key_pos, key_pos < (lane + 1) * blk).astype(_BF16)
    k_aug = jnp.concatenate([k, onehot], axis=1)
    v_aug = jnp.concatenate([v, jnp.ones((seq, dh), _BF16)], axis=1)

    qpos = lax.broadcasted_iota(jnp.int32, (blk, blk), 0)
    kpos = lax.broadcasted_iota(jnp.int32, (blk, blk), 1)
    causal = kpos <= qpos
    exp2_scale = scale * 1.4426950408889634
    for i in range(nblk):
        lo, hi = i * blk, (i + 1) * blk
        bias = jnp.transpose(bias_t[:, lo:hi]).astype(_BF16)
        q_aug = jnp.concatenate([q[lo:hi, :], bias], axis=1)
        s = _dot_nt(q_aug, k_aug[:hi, :])
        s_own = jnp.where(causal, s[:, lo:hi], MASK_VALUE)
        s = s_own if i == 0 else jnp.concatenate([s[:, :lo], s_own], axis=1)
        m = jnp.max(s, axis=-1, keepdims=True)
        p = jnp.exp2((s - m) * exp2_scale).astype(_BF16)
        o = _dot(p, v_aug[:hi, :])
        o_ref[0, lo:hi, :] = (o[:, :dh] * (1.0 / o[:, dh:dh + 1])).astype(o_ref.dtype)


def _moba_attention(qkv, heads):
    bsz, seq, three_hd = qkv.shape
    dh = three_hd // (3 * heads)
    blk = MOBA_BLOCK
    assert seq % blk == 0 and dh == LANES
    nblk = seq // blk
    topk = min(MOBA_TOPK, nblk - 1)
    kern = functools.partial(_moba_kernel, blk=blk, topk=topk, scale=dh ** -0.5)
    return pl.pallas_call(
        kern,
        grid=(bsz, heads),
        in_specs=[
            pl.BlockSpec((1, seq, dh), lambda b, h: (b, 0, h)),
            pl.BlockSpec((1, seq, dh), lambda b, h: (b, 0, heads + h)),
            pl.BlockSpec((1, seq, dh), lambda b, h: (b, 0, 2 * heads + h)),
        ],
        out_specs=pl.BlockSpec((1, seq, dh), lambda b, h: (b, 0, h)),
        out_shape=jax.ShapeDtypeStruct((bsz, seq, heads * dh), _BF16),
        compiler_params=_cparams("parallel", "parallel"),
        name="moba_attention",
    )(qkv, qkv, qkv)


def _split3(x):
    hi = x.astype(_BF16)
    r1 = x - hi.astype(_F32)
    mid = r1.astype(_BF16)
    lo = (r1 - mid.astype(_F32)).astype(_BF16)
    return hi, mid, lo


def _hgrn_kernel(x_ref, gn_ref, sh_ref, sc_ref, w_ref, lb_ref, gain_ref, o_ref, st_ref, *, heads, chunk):
    @pl.when(pl.program_id(1) == 0)
    def _init():
        st_ref[...] = jnp.zeros_like(st_ref)

    tm = x_ref.shape[1]
    nck = tm // chunk
    fdim = lb_ref.shape[1]
    kdim = fdim // heads
    vdim = gain_ref.shape[1]
    hv = heads * vdim
    h = _norm_mod(x_ref[0], gn_ref[...], sh_ref[0], sc_ref[0]).astype(_BF16)
    zq = _dot(h, w_ref[:, :fdim])
    zf = _dot(h, w_ref[:, fdim:2 * fdim])
    v = _dot(h, w_ref[:, 2 * fdim:2 * fdim + hv]).astype(_BF16)
    zg = _dot(h, w_ref[:, 2 * fdim + hv:])

    lb = lb_ref[...]
    q = zq * _sigmoid_tanh(zq)
    fgate = lb + (1.0 - lb) * _sigmoid(zf)
    k = 1.0 - fgate

    shift = chunk.bit_length() - 1
    r_i = lax.broadcasted_iota(jnp.int32, (tm, tm), 0)
    c_i = lax.broadcasted_iota(jnp.int32, (tm, tm), 1)
    block_causal = jnp.logical_and(jnp.right_shift(r_i, shift) == jnp.right_shift(c_i, shift), c_i <= r_i)
    tri = block_causal.astype(_BF16)
    b = functools.reduce(lambda a, c: a + c, [_dot(tri, part) for part in _split3(jnp.log(fgate))])

    by_chunk = lambda t: t.reshape(nck, chunk, fdim)
    b3, q3, k3 = by_chunk(b), by_chunk(q), by_chunk(k)
    b_mid = b3[:, chunk // 2:chunk // 2 + 1, :]
    b_last = b3[:, chunk - 1:chunk, :]
    flat = lambda t: t.astype(_BF16).reshape(tm, fdim)
    qd = flat(q3 * jnp.exp(b3 - b_mid))
    kd = flat(k3 * jnp.exp(b_mid - b3))
    q_in = flat(q3 * jnp.exp(b3))
    k_out = flat(k3 * jnp.exp(b_last - b3))
    decay = jnp.exp(b_last)

    gain = gain_ref[...]
    for hd in range(heads):
        ks = slice(hd * kdim, (hd + 1) * kdim)
        vs = slice(hd * vdim, (hd + 1) * vdim)
        a = jnp.where(block_causal, _dot_nt(qd[:, ks], kd[:, ks]), 0.0)
        o = _dot(a.astype(_BF16), v[:, vs])
        st = st_ref[hd]
        outs = []
        for c in range(nck):
            rows = slice(c * chunk, (c + 1) * chunk)
            outs.append(o[rows, :] + _dot_nt(q_in[rows, ks], st.astype(_BF16)))
            st = decay[c, :, ks] * st + _dot_tn(v[rows, vs], k_out[rows, ks])
        st_ref[hd] = st
        o = jnp.concatenate(outs, axis=0)
        zg_h = zg[:, vs]
        o_ref[0, :, vs] = (_rms_norm(o, gain) * (zg_h * _sigmoid_tanh(zg_h))).astype(o_ref.dtype)


def _hgrn_mixer(x, gain_mix, mod, w_in, lb, g_gain, heads, tm):
    bsz, seq, d = x.shape
    n = w_in.shape[1]
    fdim = lb.shape[-1]
    vdim = g_gain.shape[-1]
    kdim = fdim // heads
    assert kdim == LANES and vdim == LANES and n == 2 * fdim + 2 * heads * vdim
    assert HGRN_CHUNK & (HGRN_CHUNK - 1) == 0 and tm % HGRN_CHUNK == 0
    return pl.pallas_call(
        functools.partial(_hgrn_kernel, heads=heads, chunk=HGRN_CHUNK),
        grid=(bsz, seq // tm),
        in_specs=[
            pl.BlockSpec((1, tm, d), lambda b, t: (b, t, 0)),
            _resident((1, d)),
            pl.BlockSpec((1, 1, d), lambda b, t: (b, 0, 0)),
            pl.BlockSpec((1, 1, d), lambda b, t: (b, 0, 1)),
            _resident((d, n)),
            _resident((1, fdim)),
            _resident((1, vdim)),
        ],
        out_specs=pl.BlockSpec((1, tm, heads * vdim), lambda b, t: (b, t, 0)),
        out_shape=jax.ShapeDtypeStruct((bsz, seq, heads * vdim), _BF16),
        scratch_shapes=[pltpu.VMEM((heads, vdim, kdim), _F32)],
        compiler_params=_cparams("parallel", "arbitrary"),
        name="hgrn2_mixer",
    )(x, gain_mix.reshape(1, d), mod, mod, w_in, lb.reshape(1, fdim), g_gain.reshape(1, vdim))


def _gelu_tanh(x):
    return x * (0.5 * (1.0 + jnp.tanh(0.7978845608028654 * (x + 0.044715 * (x * x * x)))))


def _rglru_kernel(x_ref, gn_ref, sh_ref, sc_ref, w_ref, cw_ref, cb_ref, wa_ref, ba_ref, wi_ref, bi_ref,
                  lam_ref, o_ref, hc_ref, xh_ref, ga_ref, gu_ref, ci_ref):
    t = pl.program_id(1)

    @pl.when(t == 0)
    def _init():
        hc_ref[...] = jnp.zeros_like(hc_ref)
        xh_ref[...] = jnp.zeros_like(xh_ref)

    tm = x_ref.shape[1]
    width = o_ref.shape[2]
    hist = xh_ref.shape[0]
    h = _norm_mod(x_ref[0], gn_ref[...], sh_ref[0], sc_ref[0]).astype(_BF16)
    y_br = _gelu_tanh(_dot(h, w_ref[:, :width]))
    x_br = _dot(h, w_ref[:, width:])
    xx = jnp.concatenate([xh_ref[...], x_br], axis=0)
    xh_ref[...] = x_br[tm - hist:, :]
    x_conv = cb_ref[...]
    for j in range(RG_CONV_WIDTH):
        off = hist - (RG_CONV_WIDTH - 1) + j
        x_conv = x_conv + xx[off:off + tm, :] * cw_ref[j:j + 1, :]

    nb = wa_ref.shape[0]
    bw = width // nb
    xcb = x_conv.astype(_BF16)
    r = jnp.concatenate([_dot(xcb[:, n * bw:(n + 1) * bw], wa_ref[n]) for n in range(nb)], axis=1)
    gi = jnp.concatenate([_dot(xcb[:, n * bw:(n + 1) * bw], wi_ref[n]) for n in range(nb)], axis=1)
    r = _sigmoid_tanh(r + ba_ref[...])
    gi = _sigmoid_tanh(gi + bi_ref[...])
    neg_lam = -lam_ref[...]
    softplus = jnp.maximum(neg_lam, 0.0) + jnp.log1p(jnp.exp(-jnp.abs(neg_lam)))
    log_a = -RG_C * r * softplus
    a = jnp.exp(log_a)
    th = jnp.tanh(log_a)
    mult = jnp.sqrt(-2.0 * th / (1.0 - th))
    row = lax.broadcasted_iota(jnp.int32, (tm, 1), 0)
    mult = jnp.where(jnp.logical_and(row == 0, t == 0), 1.0, mult)
    u = (gi * x_conv) * mult

    def scan_step(a, u, pos, d):
        keep = pos >= d
        a_sh = jnp.where(keep, pltpu.roll(a, d, 0), 1.0)
        u_sh = jnp.where(keep, pltpu.roll(u, d, 0), 0.0)
        return a * a_sh, u + a * u_sh

    d = 1
    while d < SUBLANES:
        a, u = scan_step(a, u, jnp.bitwise_and(row, SUBLANES - 1), d)
        d *= 2
    ng = tm // SUBLANES
    def group_totals(ref, val):
        for j in range(width // LANES):
            ref[j] = val[:, j * LANES:(j + 1) * LANES]
        last = pl.ds(SUBLANES - 1, ng, stride=SUBLANES)
        return jnp.concatenate([ref[j, last, :] for j in range(width // LANES)], axis=1)

    ga = group_totals(ga_ref, a)
    gu = group_totals(gu_ref, u)
    grow = lax.broadcasted_iota(jnp.int32, (ng, 1), 0)
    d = 1
    while d < ng:
        ga, gu = scan_step(ga, gu, grow, d)
        d *= 2
    carry = hc_ref[...]
    h_end = gu + ga * carry
    hc_ref[...] = h_end[ng - 1:ng, :]
    ci_ref[...] = jnp.where(grow >= 1, pltpu.roll(h_end, 1, 0), carry)
    hs = jnp.concatenate(
        [u[g * SUBLANES:(g + 1) * SUBLANES, :] + a[g * SUBLANES:(g + 1) * SUBLANES, :] * ci_ref[g:g + 1, :]
         for g in range(ng)], axis=0)
    o_ref[0] = (hs * y_br).astype(o_ref.dtype)


def _rglru_mixer(x, gain_mix, mod, w_in, conv_w, conv_b, w_a, b_a, w_i, b_i, lam, tm):
    bsz, seq, d = x.shape
    two_w = w_in.shape[1]
    width = two_w // 2
    nb, bw, _ = w_a.shape
    assert tm % SUBLANES == 0
    row = lambda v: v.reshape(1, width)
    return pl.pallas_call(
        _rglru_kernel,
        grid=(bsz, seq // tm),
        in_specs=[
            pl.BlockSpec((1, tm, d), lambda b, t: (b, t, 0)),
            _resident((1, d)),
            pl.BlockSpec((1, 1, d), lambda b, t: (b, 0, 0)),
            pl.BlockSpec((1, 1, d), lambda b, t: (b, 0, 1)),
            _resident((d, two_w)),
            _resident((RG_CONV_WIDTH, width)),
            _resident((1, width)),
            _resident((nb, bw, bw)),
            _resident((1, width)),
            _resident((nb, bw, bw)),
            _resident((1, width)),
            _resident((1, width)),
        ],
        out_specs=pl.BlockSpec((1, tm, width), lambda b, t: (b, t, 0)),
        out_shape=jax.ShapeDtypeStruct((bsz, seq, width), _BF16),
        scratch_shapes=[
            pltpu.VMEM((1, width), _F32),
            pltpu.VMEM((SUBLANES, width), _F32),
            pltpu.VMEM((width // LANES, tm, LANES), _F32),
            pltpu.VMEM((width // LANES, tm, LANES), _F32),
            pltpu.VMEM((tm // SUBLANES, width), _F32),
        ],
        compiler_params=_cparams("parallel", "arbitrary"),
        name="rglru_mixer",
    )(x, gain_mix.reshape(1, d), mod, mod, w_in, conv_w, row(conv_b), w_a.astype(_BF16), row(b_a),
      w_i.astype(_BF16), row(b_i), row(lam))


def _tail_kernel(x_ref, m_ref, wo_ref, g1_ref, gn_ref, sh_ref, sc_ref, g2_ref, wup_ref, wdn_ref,
                 *rest, ff_chunk, final):
    o_ref = rest[-1]
    x1 = x_ref[0] + g1_ref[0] * _dot(m_ref[0], wo_ref[...])
    h = _norm_mod(x1, gn_ref[...], sh_ref[0], sc_ref[0]).astype(_BF16)
    acc = jnp.zeros_like(x1)
    for c in range(0, wup_ref.shape[1], ff_chunk):
        u = jnp.maximum(_dot(h, wup_ref[:, c:c + ff_chunk]), 0.0)
        acc = acc + _dot((u * u).astype(_BF16), wdn_ref[c:c + ff_chunk, :])
    x2 = x1 + g2_ref[0] * acc
    if final:
        x2 = _rms_norm(x2, rest[0][...])
    o_ref[0] = x2


def _layer_tail(x, m, w_o, mod, gain_mlp, w_up, w_down, final_gain, tm):
    bsz, seq, d = x.shape
    dm = m.shape[-1]
    dff = w_up.shape[1]
    final = final_gain is not None
    mod_spec = lambda idx: pl.BlockSpec((1, 1, d), lambda b, t: (b, 0, idx))
    in_specs = [
        pl.BlockSpec((1, tm, d), lambda b, t: (b, t, 0)),
        pl.BlockSpec((1, tm, dm), lambda b, t: (b, t, 0)),
        _resident((dm, d)),
        mod_spec(2),
        _resident((1, d)),
        mod_spec(3),
        mod_spec(4),
        mod_spec(5),
        _resident((d, dff)),
        _resident((dff, d)),
    ]
    args = [x, m, w_o, mod, gain_mlp.reshape(1, d), mod, mod, mod, w_up, w_down]
    if final:
        in_specs.append(_resident((1, d)))
        args.append(final_gain.reshape(1, d))
    return pl.pallas_call(
        functools.partial(_tail_kernel, ff_chunk=min(dff, 1024), final=final),
        grid=(bsz, seq // tm),
        in_specs=in_specs,
        out_specs=pl.BlockSpec((1, tm, d), lambda b, t: (b, t, 0)),
        out_shape=jax.ShapeDtypeStruct((bsz, seq, d), _F32),
        compiler_params=_cparams("parallel", "parallel"),
        name="layer_tail",
    )(*args)


def kernel(x, c, ada_w, ada_b, norm_mix, norm_mlp, mlp_up, mlp_down, moba_wqkv, moba_wo, hgrn_w_in, hgrn_lb, hgrn_norm, hgrn_wo, rg_w_in, rg_conv_w, rg_conv_b, rg_w_a, rg_b_a, rg_w_i, rg_b_i, rg_lambda, rg_wo, final_norm):
    depth = ada_w.shape[0]
    bsz, seq, d = x.shape
    tiles = _tiles(seq)
    bf = lambda w: w.astype(_BF16)

    mod_all = _ada_modulation(c, ada_w, ada_b)
    lb_all = _hgrn_lower_bounds(hgrn_lb)
    i_a = i_b = i_c = 0
    for layer in range(depth):
        mod = mod_all[layer].reshape(bsz, 1, 6 * d)
        kind = layer % N_MIXERS
        if kind == 0:
            qkv = _project(x, norm_mix[layer], mod, bf(moba_wqkv[i_a]), _BF16, tiles["proj"])
            m = _moba_attention(qkv, MOBA_HEADS)
            w_o = moba_wo[i_a]
            i_a += 1
        elif kind == 1:
            m = _hgrn_mixer(x, norm_mix[layer], mod, bf(hgrn_w_in[i_b]), lb_all[layer], hgrn_norm[i_b],
                            HGRN_HEADS, tiles["hgrn"])
            w_o = hgrn_wo[i_b]
            i_b += 1
        else:
            m = _rglru_mixer(x, norm_mix[layer], mod, bf(rg_w_in[i_c]), rg_conv_w[i_c], rg_conv_b[i_c],
                             rg_w_a[i_c], rg_b_a[i_c], rg_w_i[i_c], rg_b_i[i_c], rg_lambda[i_c],
                             tiles["rglru"])
            w_o = rg_wo[i_c]
            i_c += 1
        x = _layer_tail(x, m, bf(w_o), mod, norm_mlp[layer], bf(mlp_up[layer]), bf(mlp_down[layer]),
                        final_norm if layer == depth - 1 else None, tiles["tail"])
    return x
```

```python
import functools

import jax
import jax.numpy as jnp
from jax import lax
from jax.experimental import pallas as pl
from jax.experimental.pallas import tpu as pltpu

_F32 = jnp.float32
_BF16 = jnp.bfloat16
_HIGHEST = lax.Precision.HIGHEST

NORM_EPS = 1e-6
N_MIXERS = 3
MOBA_HEADS = 8
MOBA_BLOCK = 256
MOBA_TOPK = 3
HGRN_HEADS = 8
HGRN_CHUNK = 64
RG_BLOCKS = 4
RG_CONV_WIDTH = 4
RG_C = 8.0
RG_SCAN_PHASES = 8
MASK_VALUE = -1e30

V7X_VMEM_LIMIT_BYTES = 56 * 1024 * 1024
LANES = 128
SUBLANES = 8


def _tiles(seq):
    return dict(proj=min(seq, 256), tail=min(seq, 512), hgrn=min(seq, 256), rglru=min(seq, 256), moba_heads=2)


def _cparams(*sem):
    return pltpu.CompilerParams(dimension_semantics=sem, vmem_limit_bytes=V7X_VMEM_LIMIT_BYTES)


def _resident(shape):
    zeros = (0,) * len(shape)
    return pl.BlockSpec(shape, lambda *_: zeros, pipeline_mode=pl.Buffered(1))


def _sigmoid(x):
    return 1.0 / (1.0 + jnp.exp(-x))


def _sigmoid_tanh(x):
    return 0.5 * jnp.tanh(0.5 * x) + 0.5


def _rms_norm(x, gain):
    return x * lax.rsqrt(jnp.mean(x * x, axis=-1, keepdims=True) + NORM_EPS) * gain


def _norm_mod(x, gain, shift, scale):
    return _rms_norm(x, gain) * (1.0 + scale) + shift


def _dot(a, b):
    return jnp.dot(a, b, preferred_element_type=_F32)


def _dot_nt(a, b):
    return lax.dot_general(a, b, (((1,), (1,)), ((), ())), preferred_element_type=_F32)


def _dot_tn(a, b):
    return lax.dot_general(a, b, (((0,), (0,)), ((), ())), preferred_element_type=_F32)


def _ada_kernel(c_ref, w_ref, b_ref, o_ref):
    c = c_ref[...]
    cond = c * _sigmoid(c)
    o_ref[0] = jnp.dot(cond, w_ref[0], preferred_element_type=_F32, precision=_HIGHEST) + b_ref[0]


def _ada_modulation(c, ada_w, ada_b):
    depth, d, n = ada_w.shape
    bsz = c.shape[0]
    tn = min(n, 1536)
    return pl.pallas_call(
        _ada_kernel,
        grid=(depth, n // tn),
        in_specs=[
            pl.BlockSpec((bsz, d), lambda l, j: (0, 0)),
            pl.BlockSpec((1, d, tn), lambda l, j: (l, 0, j)),
            pl.BlockSpec((1, 1, tn), lambda l, j: (l, 0, j)),
        ],
        out_specs=pl.BlockSpec((1, bsz, tn), lambda l, j: (l, 0, j)),
        out_shape=jax.ShapeDtypeStruct((depth, bsz, n), _F32),
        compiler_params=_cparams("arbitrary", "arbitrary"),
        name="ada_modulation",
    )(c, ada_w, ada_b.reshape(depth, 1, n))


def _lb_kernel(x_ref, o_ref):
    depth = x_ref.shape[0]
    rows = [x_ref[l:l + 1, :] for l in range(depth)]
    mx = functools.reduce(jnp.maximum, rows)
    es = [jnp.exp(r - mx) for r in rows]
    den = functools.reduce(lambda a, b: a + b, es)
    first = es[0] / den
    run = first
    o_ref[0:1, :] = run - first
    for l in range(1, depth):
        run = run + es[l] / den
        o_ref[l:l + 1, :] = run - first


def _hgrn_lower_bounds(hgrn_lb):
    return pl.pallas_call(
        _lb_kernel,
        out_shape=jax.ShapeDtypeStruct(hgrn_lb.shape, _F32),
        name="hgrn_lower_bounds",
    )(hgrn_lb.astype(_F32))


def _proj_kernel(x_ref, g_ref, sh_ref, sc_ref, w_ref, o_ref, *, n_chunk):
    h = _norm_mod(x_ref[0], g_ref[...], sh_ref[0], sc_ref[0]).astype(_BF16)
    for c in range(0, w_ref.shape[1], n_chunk):
        o_ref[0, :, c:c + n_chunk] = _dot(h, w_ref[:, c:c + n_chunk]).astype(o_ref.dtype)


def _project(x, gain, mod, w, out_dtype, tm):
    bsz, seq, d = x.shape
    n = w.shape[1]
    return pl.pallas_call(
        functools.partial(_proj_kernel, n_chunk=min(n, 1024)),
        grid=(bsz, seq // tm),
        in_specs=[
            pl.BlockSpec((1, tm, d), lambda b, t: (b, t, 0)),
            _resident((1, d)),
            pl.BlockSpec((1, 1, d), lambda b, t: (b, 0, 0)),
            pl.BlockSpec((1, 1, d), lambda b, t: (b, 0, 1)),
            _resident((d, n)),
        ],
        out_specs=pl.BlockSpec((1, tm, n), lambda b, t: (b, t, 0)),
        out_shape=jax.ShapeDtypeStruct((bsz, seq, n), out_dtype),
        compiler_params=_cparams("parallel", "parallel"),
        name="project",
    )(x, gain.reshape(1, d), mod, mod, w)


def _moba_kernel(q_ref, k_ref, v_ref, o_ref, *, blk, topk, scale):
    dh = LANES
    for hd in range(q_ref.shape[2] // dh):
        cols = slice(hd * dh, (hd + 1) * dh)
        _moba_head(q_ref[0, :, cols], k_ref[0, :, cols], v_ref[0, :, cols], o_ref, cols,
                   blk=blk, topk=topk, scale=scale)


def _moba_head(q, k, v, o_ref, cols, *, blk, topk, scale):
    seq, dh = k.shape
    nblk = seq // blk

    row = lax.broadcasted_iota(jnp.int32, (nblk, seq), 0)
    col = lax.broadcasted_iota(jnp.int32, (nblk, seq), 1)
    fully_past = (row + 1) * blk <= col
    kmean = jnp.sum(k.astype(_F32).reshape(nblk, blk, dh), axis=1) * (1.0 / blk)
    kmean_hi = kmean.astype(_BF16)
    kmean_lo = (kmean - kmean_hi.astype(_F32)).astype(_BF16)
    gate_parts = _dot_nt(jnp.concatenate([kmean_hi, kmean_lo], axis=0), q)
    gate_t = gate_parts[:nblk, :] + gate_parts[nblk:, :]

    bias_rows = []
    for j in range(nblk):
        g_j = gate_t[j:j + 1, :]
        beats = jnp.logical_and(
            fully_past, jnp.logical_or(gate_t > g_j, jnp.logical_and(gate_t == g_j, row < j)))
        rank = jnp.sum(beats.astype(_F32), axis=0, keepdims=True)
        masked = jnp.logical_and(fully_past[j:j + 1, :], rank >= topk)
        bias_rows.append(jnp.where(masked, MASK_VALUE, 0.0))
    bias_t = jnp.concatenate(bias_rows + [jnp.zeros((dh - nblk, seq), _F32)], axis=0)

    key_pos = lax.broadcasted_iota(jnp.int32, (seq, dh), 0)
    lane = lax.broadcasted_iota(jnp.int32, (seq, dh), 1)
    onehot = jnp.logical_and(lane * blk <= key_pos, key_pos < (lane + 1) * blk).astype(_BF16)
    k_aug = jnp.concatenate([k, onehot], axis=1)
    v_aug = jnp.concatenate([v, jnp.ones((seq, dh), _BF16)], axis=1)

    qpos = lax.broadcasted_iota(jnp.int32, (blk, blk), 0)
    kpos = lax.broadcasted_iota(jnp.int32, (blk, blk), 1)
    causal = kpos <= qpos
    exp2_scale = scale * 1.4426950408889634
    for i in [0] + list(range(nblk - 1, 0, -1)):
        lo, hi = i * blk, (i + 1) * blk
        if i == 0:
            bias = jnp.zeros((blk, dh), _BF16)
        else:
            bias = jnp.transpose(bias_t[:, lo:hi]).astype(_BF16)
        q_aug = jnp.concatenate([q[lo:hi, :], bias], axis=1)
        s = _dot_nt(q_aug, k_aug[:hi, :])
        s_own = jnp.where(causal, s[:, lo:hi], MASK_VALUE)
        s = s_own if i == 0 else jnp.concatenate([s[:, :lo], s_own], axis=1)
        m = jnp.max(s, axis=-1, keepdims=True)
        p = jnp.exp2((s - m) * exp2_scale).astype(_BF16)
        o = _dot(p, v_aug[:hi, :])
        o_ref[0, lo:hi, cols] = (o[:, :dh] * (1.0 / o[:, dh:dh + 1])).astype(o_ref.dtype)


def _moba_attention(qkv, heads, heads_per_step):
    bsz, seq, three_hd = qkv.shape
    dh = three_hd // (3 * heads)
    blk = MOBA_BLOCK
    assert seq % blk == 0 and dh == LANES and heads % heads_per_step == 0
    nblk = seq // blk
    topk = min(MOBA_TOPK, nblk - 1)
    groups = heads // heads_per_step
    width = heads_per_step * dh
    kern = functools.partial(_moba_kernel, blk=blk, topk=topk, scale=dh ** -0.5)
    return pl.pallas_call(
        kern,
        grid=(bsz, groups),
        in_specs=[
            pl.BlockSpec((1, seq, width), lambda b, g: (b, 0, g)),
            pl.BlockSpec((1, seq, width), lambda b, g: (b, 0, groups + g)),
            pl.BlockSpec((1, seq, width), lambda b, g: (b, 0, 2 * groups + g)),
        ],
        out_specs=pl.BlockSpec((1, seq, width), lambda b, g: (b, 0, g)),
        out_shape=jax.ShapeDtypeStruct((bsz, seq, heads * dh), _BF16),
        compiler_params=_cparams("parallel", "parallel"),
        name="moba_attention",
    )(qkv, qkv, qkv)


def _split3(x):
    hi = x.astype(_BF16)
    r1 = x - hi.astype(_F32)
    mid = r1.astype(_BF16)
    lo = (r1 - mid.astype(_F32)).astype(_BF16)
    return hi, mid, lo


def _hgrn_kernel(x_ref, gn_ref, sh_ref, sc_ref, w_ref, lb_ref, gain_ref, o_ref, st_ref, *, heads, chunk):
    @pl.when(pl.program_id(1) == 0)
    def _init():
        st_ref[...] = jnp.zeros_like(st_ref)

    tm = x_ref.shape[1]
    nck = tm // chunk
    fdim = lb_ref.shape[1]
    kdim = fdim // heads
    vdim = gain_ref.shape[1]
    hv = heads * vdim
    h = _norm_mod(x_ref[0], gn_ref[...], sh_ref[0], sc_ref[0]).astype(_BF16)
    zq = _dot(h, w_ref[:, :fdim])
    zf = _dot(h, w_ref[:, fdim:2 * fdim])
    v = _dot(h, w_ref[:, 2 * fdim:2 * fdim + hv]).astype(_BF16)
    zg = _dot(h, w_ref[:, 2 * fdim + hv:])

    lb = lb_ref[...]
    q = zq * _sigmoid_tanh(zq)
    fgate = lb + (1.0 - lb) * _sigmoid(zf)
    k = 1.0 - fgate

    shift = chunk.bit_length() - 1
    r_i = lax.broadcasted_iota(jnp.int32, (tm, tm), 0)
    c_i = lax.broadcasted_iota(jnp.int32, (tm, tm), 1)
    block_causal = jnp.logical_and(jnp.right_shift(r_i, shift) == jnp.right_shift(c_i, shift), c_i <= r_i)
    tri = block_causal.astype(_BF16)
    b = functools.reduce(lambda a, c: a + c, [_dot(tri, part) for part in _split3(jnp.log(fgate))])

    by_chunk = lambda t: t.reshape(nck, chunk, fdim)
    b3, q3, k3 = by_chunk(b), by_chunk(q), by_chunk(k)
    b_mid = b3[:, chunk // 2:chunk // 2 + 1, :]
    b_last = b3[:, chunk - 1:chunk, :]
    flat = lambda t: t.astype(_BF16).reshape(tm, fdim)
    qd = flat(q3 * jnp.exp(b3 - b_mid))
    kd = flat(k3 * jnp.exp(b_mid - b3))
    q_in = flat(q3 * jnp.exp(b3))
    k_out = flat(k3 * jnp.exp(b_last - b3))
    decay = jnp.exp(b_last)

    gain = gain_ref[...]
    for hd in range(heads):
        ks = slice(hd * kdim, (hd + 1) * kdim)
        vs = slice(hd * vdim, (hd + 1) * vdim)
        a = jnp.where(block_causal, _dot_nt(qd[:, ks], kd[:, ks]), 0.0)
        o = _dot(a.astype(_BF16), v[:, vs])
        st = st_ref[hd]
        outs = []
        for c in range(nck):
            rows = slice(c * chunk, (c + 1) * chunk)
            outs.append(o[rows, :] + _dot_nt(q_in[rows, ks], st.astype(_BF16)))
            st = decay[c, :, ks] * st + _dot_tn(v[rows, vs], k_out[rows, ks])
        st_ref[hd] = st
        o = jnp.concatenate(outs, axis=0)
        zg_h = zg[:, vs]
        o_ref[0, :, vs] = (_rms_norm(o, gain) * (zg_h * _sigmoid_tanh(zg_h))).astype(o_ref.dtype)


def _hgrn_mixer(x, gain_mix, mod, w_in, lb, g_gain, heads, tm):
    bsz, seq, d = x.shape
    n = w_in.shape[1]
    fdim = lb.shape[-1]
    vdim = g_gain.shape[-1]
    kdim = fdim // heads
    assert kdim == LANES and vdim == LANES and n == 2 * fdim + 2 * heads * vdim
    assert HGRN_CHUNK & (HGRN_CHUNK - 1) == 0 and tm % HGRN_CHUNK == 0
    return pl.pallas_call(
        functools.partial(_hgrn_kernel, heads=heads, chunk=HGRN_CHUNK),
        grid=(bsz, seq // tm),
        in_specs=[
            pl.BlockSpec((1, tm, d), lambda b, t: (b, t, 0)),
            _resident((1, d)),
            pl.BlockSpec((1, 1, d), lambda b, t: (b, 0, 0)),
            pl.BlockSpec((1, 1, d), lambda b, t: (b, 0, 1)),
            _resident((d, n)),
            _resident((1, fdim)),
            _resident((1, vdim)),
        ],
        out_specs=pl.BlockSpec((1, tm, heads * vdim), lambda b, t: (b, t, 0)),
        out_shape=jax.ShapeDtypeStruct((bsz, seq, heads * vdim), _BF16),
        scratch_shapes=[pltpu.VMEM((heads, vdim, kdim), _F32)],
        compiler_params=_cparams("parallel", "arbitrary"),
        name="hgrn2_mixer",
    )(x, gain_mix.reshape(1, d), mod, mod, w_in, lb.reshape(1, fdim), g_gain.reshape(1, vdim))


def _gelu_tanh(x):
    return x * (0.5 * (1.0 + jnp.tanh(0.7978845608028654 * (x + 0.044715 * (x * x * x)))))


def _rglru_kernel(x_ref, gn_ref, sh_ref, sc_ref, w_ref, cw_ref, cb_ref, wa_ref, ba_ref, wi_ref, bi_ref,
                  lam_ref, o_ref, hc_ref, xh_ref, *, phases):
    t = pl.program_id(1)

    @pl.when(t == 0)
    def _init():
        hc_ref[...] = jnp.zeros_like(hc_ref)
        xh_ref[...] = jnp.zeros_like(xh_ref)

    tm = x_ref.shape[1]
    width = o_ref.shape[2]
    ng = tm // phases
    shift = ng.bit_length() - 1
    i_r = lax.broadcasted_iota(jnp.int32, (tm, tm), 0)
    i_c = lax.broadcasted_iota(jnp.int32, (tm, tm), 1)
    time_of = lambda i: jnp.bitwise_and(i, ng - 1) * phases + jnp.right_shift(i, shift)
    perm = (i_c == time_of(i_r)).astype(_BF16)
    unperm = (i_r == time_of(i_c)).astype(_BF16)

    h = _norm_mod(x_ref[0], gn_ref[...], sh_ref[0], sc_ref[0]).astype(_BF16)
    h = _dot(perm, h).astype(_BF16)
    y_br = _gelu_tanh(_dot(h, w_ref[:, :width]))
    x_br = _dot(h, w_ref[:, width:])

    slab = lambda val, p: val[p * ng:(p + 1) * ng, :]
    grow = lax.broadcasted_iota(jnp.int32, (ng, 1), 0)
    taps = cw_ref.shape[0]
    prev_group = {p: jnp.where(grow >= 1, pltpu.roll(slab(x_br, p), 1, 0), xh_ref[p:p + 1, :])
                  for p in range(phases - (taps - 1), phases)}
    for p in range(phases - (taps - 1), phases):
        xh_ref[p:p + 1, :] = slab(x_br, p)[ng - 1:ng, :]
    conv_slabs = []
    for p in range(phases):
        acc = cb_ref[...]
        for j in range(taps):
            src = p - (taps - 1) + j
            acc = acc + (slab(x_br, src) if src >= 0 else prev_group[src + phases]) * cw_ref[j:j + 1, :]
        conv_slabs.append(acc)
    x_conv = jnp.concatenate(conv_slabs, axis=0)

    nb = wa_ref.shape[0]
    bw = width // nb
    xcb = x_conv.astype(_BF16)
    r = jnp.concatenate([_dot(xcb[:, n * bw:(n + 1) * bw], wa_ref[n]) for n in range(nb)], axis=1)
    gi = jnp.concatenate([_dot(xcb[:, n * bw:(n + 1) * bw], wi_ref[n]) for n in range(nb)], axis=1)
    r = _sigmoid_tanh(r + ba_ref[...])
    gi = _sigmoid_tanh(gi + bi_ref[...])
    neg_lam = -lam_ref[...]
    softplus = jnp.maximum(neg_lam, 0.0) + jnp.log1p(jnp.exp(-jnp.abs(neg_lam)))
    log_a = -RG_C * r * softplus
    a = jnp.exp(log_a)
    th = jnp.tanh(log_a)
    mult = jnp.sqrt(-2.0 * th) * lax.rsqrt(1.0 - th)
    row = lax.broadcasted_iota(jnp.int32, (tm, 1), 0)
    mult = jnp.where(jnp.logical_and(row == 0, t == 0), 1.0, mult)
    u = (gi * x_conv) * mult

    h_loc, a_cum = [slab(u, 0)], [slab(a, 0)]
    for p in range(1, phases):
        h_loc.append(slab(a, p) * h_loc[-1] + slab(u, p))
        a_cum.append(slab(a, p) * a_cum[-1])
    ga, gu = a_cum[-1], h_loc[-1]
    d = 1
    while d < ng:
        keep = grow >= d
        a_sh = jnp.where(keep, pltpu.roll(ga, d, 0), 1.0)
        u_sh = jnp.where(keep, pltpu.roll(gu, d, 0), 0.0)
        ga, gu = ga * a_sh, gu + ga * u_sh
        d *= 2
    carry = hc_ref[...]
    h_end = gu + ga * carry
    hc_ref[...] = h_end[ng - 1:ng, :]
    h_in = jnp.where(grow >= 1, pltpu.roll(h_end, 1, 0), carry)
    hs = jnp.concatenate([h_loc[p] + a_cum[p] * h_in for p in range(phases)], axis=0)
    o_ref[0] = _dot(unperm, (hs * y_br).astype(_BF16)).astype(o_ref.dtype)


def _rglru_mixer(x, gain_mix, mod, w_in, conv_w, conv_b, w_a, b_a, w_i, b_i, lam, tm):
    bsz, seq, d = x.shape
    two_w = w_in.shape[1]
    width = two_w // 2
    nb, bw, _ = w_a.shape
    phases = RG_SCAN_PHASES
    ng = tm // phases
    assert tm % phases == 0 and ng % SUBLANES == 0 and ng & (ng - 1) == 0 and phases >= RG_CONV_WIDTH
    row = lambda v: v.reshape(1, width)
    return pl.pallas_call(
        functools.partial(_rglru_kernel, phases=phases),
        grid=(bsz, seq // tm),
        in_specs=[
            pl.BlockSpec((1, tm, d), lambda b, t: (b, t, 0)),
            _resident((1, d)),
            pl.BlockSpec((1, 1, d), lambda b, t: (b, 0, 0)),
            pl.BlockSpec((1, 1, d), lambda b, t: (b, 0, 1)),
            _resident((d, two_w)),
            _resident((RG_CONV_WIDTH, width)),
            _resident((1, width)),
            _resident((nb, bw, bw)),
            _resident((1, width)),
            _resident((nb, bw, bw)),
            _resident((1, width)),
            _resident((1, width)),
        ],
        out_specs=pl.BlockSpec((1, tm, width), lambda b, t: (b, t, 0)),
        out_shape=jax.ShapeDtypeStruct((bsz, seq, width), _BF16),
        scratch_shapes=[
            pltpu.VMEM((1, width), _F32),
            pltpu.VMEM((phases, width), _F32),
        ],
        compiler_params=_cparams("parallel", "arbitrary"),
        name="rglru_mixer",
    )(x, gain_mix.reshape(1, d), mod, mod, w_in, conv_w, row(conv_b), w_a.astype(_BF16), row(b_a),
      w_i.astype(_BF16), row(b_i), row(lam))


def _tail_kernel(x_ref, m_ref, wo_ref, g1_ref, gn_ref, sh_ref, sc_ref, g2_ref, wup_ref, wdn_ref,
                 *rest, ff_chunk, final):
    o_ref = rest[-1]
    x1 = x_ref[0] + g1_ref[0] * _dot(m_ref[0], wo_ref[...])
    h = _norm_mod(x1, gn_ref[...], sh_ref[0], sc_ref[0]).astype(_BF16)
    acc = jnp.zeros_like(x1)
    for c in range(0, wup_ref.shape[1], ff_chunk):
        u = jnp.maximum(_dot(h, wup_ref[:, c:c + ff_chunk]), 0.0)
        acc = acc + _dot((u * u).astype(_BF16), wdn_ref[c:c + ff_chunk, :])
    x2 = x1 + g2_ref[0] * acc
    if final:
        x2 = _rms_norm(x2, rest[0][...])
    o_ref[0] = x2


def _layer_tail(x, m, w_o, mod, gain_mlp, w_up, w_down, final_gain, tm):
    bsz, seq, d = x.shape
    dm = m.shape[-1]
    dff = w_up.shape[1]
    final = final_gain is not None
    mod_spec = lambda idx: pl.BlockSpec((1, 1, d), lambda b, t: (b, 0, idx))
    in_specs = [
        pl.BlockSpec((1, tm, d), lambda b, t: (b, t, 0)),
        pl.BlockSpec((1, tm, dm), lambda b, t: (b, t, 0)),
        _resident((dm, d)),
        mod_spec(2),
        _resident((1, d)),
        mod_spec(3),
        mod_spec(4),
        mod_spec(5),
        _resident((d, dff)),
        _resident((dff, d)),
    ]
    args = [x, m, w_o, mod, gain_mlp.reshape(1, d), mod, mod, mod, w_up, w_down]
    if final:
        in_specs.append(_resident((1, d)))
        args.append(final_gain.reshape(1, d))
    return pl.pallas_call(
        functools.partial(_tail_kernel, ff_chunk=min(dff, 1024), final=final),
        grid=(bsz, seq // tm),
        in_specs=in_specs,
        out_specs=pl.BlockSpec((1, tm, d), lambda b, t: (b, t, 0)),
        out_shape=jax.ShapeDtypeStruct((bsz, seq, d), _F32),
        compiler_params=_cparams("parallel", "parallel"),
        name="layer_tail",
    )(*args)


def kernel(x, c, ada_w, ada_b, norm_mix, norm_mlp, mlp_up, mlp_down, moba_wqkv, moba_wo, hgrn_w_in, hgrn_lb, hgrn_norm, hgrn_wo, rg_w_in, rg_conv_w, rg_conv_b, rg_w_a, rg_b_a, rg_w_i, rg_b_i, rg_lambda, rg_wo, final_norm):
    depth = ada_w.shape[0]
    bsz, seq, d = x.shape
    tiles = _tiles(seq)
    bf = lambda w: w.astype(_BF16)

    mod_all = _ada_modulation(c, ada_w, ada_b)
    lb_all = _hgrn_lower_bounds(hgrn_lb)
    i_a = i_b = i_c = 0
    for layer in range(depth):
        mod = mod_all[layer].reshape(bsz, 1, 6 * d)
        kind = layer % N_MIXERS
        if kind == 0:
            qkv = _project(x, norm_mix[layer], mod, bf(moba_wqkv[i_a]), _BF16, tiles["proj"])
            m = _moba_attention(qkv, MOBA_HEADS, tiles["moba_heads"])
            w_o = moba_wo[i_a]
            i_a += 1
        elif kind == 1:
            m = _hgrn_mixer(x, norm_mix[layer], mod, bf(hgrn_w_in[i_b]), lb_all[layer], hgrn_norm[i_b],
                            HGRN_HEADS, tiles["hgrn"])
            w_o = hgrn_wo[i_b]
            i_b += 1
        else:
            m = _rglru_mixer(x, norm_mix[layer], mod, bf(rg_w_in[i_c]), rg_conv_w[i_c], rg_conv_b[i_c],
                             rg_w_a[i_c], rg_b_a[i_c], rg_w_i[i_c], rg_b_i[i_c], rg_lambda[i_c],
                             tiles["rglru"])
            w_o = rg_wo[i_c]
            i_c += 1
        x = _layer_tail(x, m, bf(w_o), mod, norm_mlp[layer], bf(mlp_up[layer]), bf(mlp_down[layer]),
                        final_norm if layer == depth - 1 else None, tiles["tail"])
    return x
```

```python
import functools

import jax
import jax.numpy as jnp
from jax import lax
from jax.experimental import pallas as pl
from jax.experimental.pallas import tpu as pltpu

_F32 = jnp.float32
_BF16 = jnp.bfloat16
_HIGHEST = lax.Precision.HIGHEST

NORM_EPS = 1e-6
N_MIXERS = 3
MOBA_HEADS = 8
MOBA_BLOCK = 256
MOBA_TOPK = 3
HGRN_HEADS = 8
HGRN_CHUNK = 64
HGRN_SUBTILE = 256
RG_BLOCKS = 4
RG_CONV_WIDTH = 4
RG_C = 8.0
RG_SUBTILE = 256
RG_SCAN_PHASES = 8
MASK_VALUE = -1e30

V7X_VMEM_LIMIT_BYTES = 56 * 1024 * 1024
LANES = 128
SUBLANES = 8


def _tiles(seq):
    return dict(proj=min(seq, 512), tail=min(seq, 512), hgrn=min(seq, 512), rglru=min(seq, 512), moba_heads=2)


def _cparams(*sem):
    return pltpu.CompilerParams(dimension_semantics=sem, vmem_limit_bytes=V7X_VMEM_LIMIT_BYTES)


def _resident(shape):
    zeros = (0,) * len(shape)
    return pl.BlockSpec(shape, lambda *_: zeros, pipeline_mode=pl.Buffered(1))


def _sigmoid(x):
    return 1.0 / (1.0 + jnp.exp(-x))


def _sigmoid_tanh(x):
    return 0.5 * jnp.tanh(0.5 * x) + 0.5


def _rms_norm(x, gain):
    return x * lax.rsqrt(jnp.mean(x * x, axis=-1, keepdims=True) + NORM_EPS) * gain


def _norm_mod(x, gain, shift, scale):
    return _rms_norm(x, gain) * (1.0 + scale) + shift


def _dot(a, b):
    return jnp.dot(a, b, preferred_element_type=_F32)


def _dot_nt(a, b):
    return lax.dot_general(a, b, (((1,), (1,)), ((), ())), preferred_element_type=_F32)


def _dot_tn(a, b):
    return lax.dot_general(a, b, (((0,), (0,)), ((), ())), preferred_element_type=_F32)


def _ada_kernel(c_ref, w_ref, b_ref, o_ref):
    c = c_ref[...]
    cond = c * _sigmoid(c)
    o_ref[0] = jnp.dot(cond, w_ref[0], preferred_element_type=_F32, precision=_HIGHEST) + b_ref[0]


def _ada_modulation(c, ada_w, ada_b):
    depth, d, n = ada_w.shape
    bsz = c.shape[0]
    tn = min(n, 1536)
    return pl.pallas_call(
        _ada_kernel,
        grid=(depth, n // tn),
        in_specs=[
            pl.BlockSpec((bsz, d), lambda l, j: (0, 0)),
            pl.BlockSpec((1, d, tn), lambda l, j: (l, 0, j)),
            pl.BlockSpec((1, 1, tn), lambda l, j: (l, 0, j)),
        ],
        out_specs=pl.BlockSpec((1, bsz, tn), lambda l, j: (l, 0, j)),
        out_shape=jax.ShapeDtypeStruct((depth, bsz, n), _F32),
        compiler_params=_cparams("arbitrary", "arbitrary"),
        name="ada_modulation",
    )(c, ada_w, ada_b.reshape(depth, 1, n))


def _lb_kernel(x_ref, o_ref):
    depth = x_ref.shape[0]
    rows = [x_ref[l:l + 1, :] for l in range(depth)]
    mx = functools.reduce(jnp.maximum, rows)
    es = [jnp.exp(r - mx) for r in rows]
    den = functools.reduce(lambda a, b: a + b, es)
    first = es[0] / den
    run = first
    o_ref[0:1, :] = run - first
    for l in range(1, depth):
        run = run + es[l] / den
        o_ref[l:l + 1, :] = run - first


def _hgrn_lower_bounds(hgrn_lb):
    return pl.pallas_call(
        _lb_kernel,
        out_shape=jax.ShapeDtypeStruct(hgrn_lb.shape, _F32),
        name="hgrn_lower_bounds",
    )(hgrn_lb.astype(_F32))


def _proj_kernel(x_ref, g_ref, sh_ref, sc_ref, w_ref, o_ref, *, n_chunk):
    h = _norm_mod(x_ref[0], g_ref[...], sh_ref[0], sc_ref[0]).astype(_BF16)
    for c in range(0, w_ref.shape[1], n_chunk):
        o_ref[0, :, c:c + n_chunk] = _dot(h, w_ref[:, c:c + n_chunk]).astype(o_ref.dtype)


def _project(x, gain, mod, w, out_dtype, tm):
    bsz, seq, d = x.shape
    n = w.shape[1]
    return pl.pallas_call(
        functools.partial(_proj_kernel, n_chunk=min(n, 1024)),
        grid=(bsz, seq // tm),
        in_specs=[
            pl.BlockSpec((1, tm, d), lambda b, t: (b, t, 0)),
            _resident((1, d)),
            pl.BlockSpec((1, 1, d), lambda b, t: (b, 0, 0)),
            pl.BlockSpec((1, 1, d), lambda b, t: (b, 0, 1)),
            _resident((d, n)),
        ],
        out_specs=pl.BlockSpec((1, tm, n), lambda b, t: (b, t, 0)),
        out_shape=jax.ShapeDtypeStruct((bsz, seq, n), out_dtype),
        compiler_params=_cparams("parallel", "parallel"),
        name="project",
    )(x, gain.reshape(1, d), mod, mod, w)


def _moba_kernel(q_ref, k_ref, v_ref, o_ref, *, blk, topk, scale):
    dh = LANES
    for hd in range(q_ref.shape[2] // dh):
        cols = slice(hd * dh, (hd + 1) * dh)
        _moba_head(q_ref[0, :, cols], k_ref[0, :, cols], v_ref[0, :, cols], o_ref, cols,
                   blk=blk, topk=topk, scale=scale)


def _moba_head(q, k, v, o_ref, cols, *, blk, topk, scale):
    seq, dh = k.shape
    nblk = seq // blk

    row = lax.broadcasted_iota(jnp.int32, (nblk, seq), 0)
    col = lax.broadcasted_iota(jnp.int32, (nblk, seq), 1)
    fully_past = (row + 1) * blk <= col
    kmean = jnp.sum(k.astype(_F32).reshape(nblk, blk, dh), axis=1) * (1.0 / blk)
    kmean_hi = kmean.astype(_BF16)
    kmean_lo = (kmean - kmean_hi.astype(_F32)).astype(_BF16)
    gate_parts = _dot_nt(jnp.concatenate([kmean_hi, kmean_lo], axis=0), q)
    gate_t = gate_parts[:nblk, :] + gate_parts[nblk:, :]

    bias_rows = []
    for j in range(nblk):
        g_j = gate_t[j:j + 1, :]
        beats = jnp.logical_and(
            fully_past, jnp.logical_or(gate_t > g_j, jnp.logical_and(gate_t == g_j, row < j)))
        rank = jnp.sum(beats.astype(_F32), axis=0, keepdims=True)
        masked = jnp.logical_and(fully_past[j:j + 1, :], rank >= topk)
        bias_rows.append(jnp.where(masked, MASK_VALUE, 0.0))
    bias_t = jnp.concatenate(bias_rows + [jnp.zeros((dh - nblk, seq), _F32)], axis=0)

    key_pos = lax.broadcasted_iota(jnp.int32, (seq, dh), 0)
    lane = lax.broadcasted_iota(jnp.int32, (seq, dh), 1)
    onehot = jnp.logical_and(lane * blk <= key_pos, key_pos < (lane + 1) * blk).astype(_BF16)
    k_aug = jnp.concatenate([k, onehot], axis=1)
    v_aug = jnp.concatenate([v, jnp.ones((seq, dh), _BF16)], axis=1)

    qpos = lax.broadcasted_iota(jnp.int32, (blk, blk), 0)
    kpos = lax.broadcasted_iota(jnp.int32, (blk, blk), 1)
    causal = kpos <= qpos
    exp2_scale = scale * 1.4426950408889634
    order = [0] + list(range(nblk - 1, 0, -1))
    logits = {}
    for i in order:
        lo, hi = i * blk, (i + 1) * blk
        if i == 0:
            bias = jnp.zeros((blk, dh), _BF16)
        else:
            bias = jnp.transpose(bias_t[:, lo:hi]).astype(_BF16)
        q_aug = jnp.concatenate([q[lo:hi, :], bias], axis=1)
        logits[i] = _dot_nt(q_aug, k_aug[:hi, :])
    probs = {}
    for i in order:
        lo, hi = i * blk, (i + 1) * blk
        s = logits[i]
        s_own = jnp.where(causal, s[:, lo:hi], MASK_VALUE)
        s = s_own if i == 0 else jnp.concatenate([s[:, :lo], s_own], axis=1)
        m = jnp.max(s, axis=-1, keepdims=True)
        probs[i] = jnp.exp2((s - m) * exp2_scale).astype(_BF16)
    for i in order:
        lo, hi = i * blk, (i + 1) * blk
        o = _dot(probs[i], v_aug[:hi, :])
        o_ref[0, lo:hi, cols] = (o[:, :dh] * (1.0 / o[:, dh:dh + 1])).astype(o_ref.dtype)


def _moba_attention(qkv, heads, heads_per_step):
    bsz, seq, three_hd = qkv.shape
    dh = three_hd // (3 * heads)
    blk = MOBA_BLOCK
    assert seq % blk == 0 and dh == LANES and heads % heads_per_step == 0
    nblk = seq // blk
    topk = min(MOBA_TOPK, nblk - 1)
    groups = heads // heads_per_step
    width = heads_per_step * dh
    kern = functools.partial(_moba_kernel, blk=blk, topk=topk, scale=dh ** -0.5)
    return pl.pallas_call(
        kern,
        grid=(bsz, groups),
        in_specs=[
            pl.BlockSpec((1, seq, width), lambda b, g: (b, 0, g)),
            pl.BlockSpec((1, seq, width), lambda b, g: (b, 0, groups + g)),
            pl.BlockSpec((1, seq, width), lambda b, g: (b, 0, 2 * groups + g)),
        ],
        out_specs=pl.BlockSpec((1, seq, width), lambda b, g: (b, 0, g)),
        out_shape=jax.ShapeDtypeStruct((bsz, seq, heads * dh), _BF16),
        compiler_params=_cparams("parallel", "parallel"),
        name="moba_attention",
    )(qkv, qkv, qkv)


def _split3(x):
    hi = x.astype(_BF16)
    r1 = x - hi.astype(_F32)
    mid = r1.astype(_BF16)
    lo = (r1 - mid.astype(_F32)).astype(_BF16)
    return hi, mid, lo


def _hgrn_kernel(x_ref, gn_ref, sh_ref, sc_ref, w_ref, lb_ref, gain_ref, o_ref, st_ref, *, heads, chunk, sub):
    @pl.when(pl.program_id(1) == 0)
    def _init():
        st_ref[...] = jnp.zeros_like(st_ref)

    fdim = lb_ref.shape[1]
    hv = heads * gain_ref.shape[1]
    tiles = [slice(r, r + sub) for r in range(0, x_ref.shape[1], sub)]

    def project(rows):
        h = _norm_mod(x_ref[0, rows, :], gn_ref[...], sh_ref[0], sc_ref[0]).astype(_BF16)
        return (_dot(h, w_ref[:, :fdim]), _dot(h, w_ref[:, fdim:2 * fdim]),
                _dot(h, w_ref[:, 2 * fdim:2 * fdim + hv]).astype(_BF16), _dot(h, w_ref[:, 2 * fdim + hv:]))

    z = project(tiles[0])
    for i, rows in enumerate(tiles):
        gates = _hgrn_gates(z[0], z[1], lb_ref[...], chunk=chunk)
        v, zg = z[2], z[3]
        if i + 1 < len(tiles):
            z = project(tiles[i + 1])
        _hgrn_heads(gates, v, zg, gain_ref[...], o_ref, rows, st_ref, heads=heads, chunk=chunk)


def _hgrn_gates(zq, zf, lb, *, chunk):
    tm, fdim = zq.shape
    nck = tm // chunk
    q = zq * _sigmoid_tanh(zq)
    fgate = lb + (1.0 - lb) * _sigmoid(zf)
    k = 1.0 - fgate

    shift = chunk.bit_length() - 1
    r_i = lax.broadcasted_iota(jnp.int32, (tm, tm), 0)
    c_i = lax.broadcasted_iota(jnp.int32, (tm, tm), 1)
    block_causal = jnp.logical_and(jnp.right_shift(r_i, shift) == jnp.right_shift(c_i, shift), c_i <= r_i)
    tri = block_causal.astype(_BF16)
    b = functools.reduce(lambda a, c: a + c, [_dot(tri, part) for part in _split3(jnp.log(fgate))])

    by_chunk = lambda t: t.reshape(nck, chunk, fdim)
    b3, q3, k3 = by_chunk(b), by_chunk(q), by_chunk(k)
    b_mid = b3[:, chunk // 2:chunk // 2 + 1, :]
    b_last = b3[:, chunk - 1:chunk, :]
    flat = lambda t: t.astype(_BF16).reshape(tm, fdim)
    qd = flat(q3 * jnp.exp(b3 - b_mid))
    kd = flat(k3 * jnp.exp(b_mid - b3))
    q_in = flat(q3 * jnp.exp(b3))
    k_out = flat(k3 * jnp.exp(b_last - b3))
    decay = jnp.exp(b_last)
    return qd, kd, q_in, k_out, decay, block_causal


def _hgrn_heads(gates, v, zg, gain, o_ref, rows_out, st_ref, *, heads, chunk):
    qd, kd, q_in, k_out, decay, block_causal = gates
    tm, fdim = qd.shape
    nck = tm // chunk
    kdim = fdim // heads
    vdim = gain.shape[1]
    ksl = [slice(hd * kdim, (hd + 1) * kdim) for hd in range(heads)]
    vsl = [slice(hd * vdim, (hd + 1) * vdim) for hd in range(heads)]
    crow = [slice(c * chunk, (c + 1) * chunk) for c in range(nck)]
    a_all = [jnp.where(block_causal, _dot_nt(qd[:, ks], kd[:, ks]), 0.0).astype(_BF16) for ks in ksl]
    o_all = [_dot(a_all[hd], v[:, vsl[hd]]) for hd in range(heads)]
    upd = [[_dot_tn(v[rows, vsl[hd]], k_out[rows, ksl[hd]]) for rows in crow] for hd in range(heads)]
    for hd in range(heads):
        st = st_ref[hd]
        outs = []
        for c in range(nck):
            outs.append(o_all[hd][crow[c], :] + _dot_nt(q_in[crow[c], ksl[hd]], st.astype(_BF16)))
            st = decay[c, :, ksl[hd]] * st + upd[hd][c]
        st_ref[hd] = st
        o = jnp.concatenate(outs, axis=0)
        zg_h = zg[:, vsl[hd]]
        o_ref[0, rows_out, vsl[hd]] = (_rms_norm(o, gain) * (zg_h * _sigmoid_tanh(zg_h))).astype(o_ref.dtype)


def _hgrn_mixer(x, gain_mix, mod, w_in, lb, g_gain, heads, tm):
    bsz, seq, d = x.shape
    n = w_in.shape[1]
    fdim = lb.shape[-1]
    vdim = g_gain.shape[-1]
    kdim = fdim // heads
    sub = min(tm, HGRN_SUBTILE)
    assert kdim == LANES and vdim == LANES and n == 2 * fdim + 2 * heads * vdim
    assert HGRN_CHUNK & (HGRN_CHUNK - 1) == 0 and sub % HGRN_CHUNK == 0 and tm % sub == 0
    return pl.pallas_call(
        functools.partial(_hgrn_kernel, heads=heads, chunk=HGRN_CHUNK, sub=sub),
        grid=(bsz, seq // tm),
        in_specs=[
            pl.BlockSpec((1, tm, d), lambda b, t: (b, t, 0)),
            _resident((1, d)),
            pl.BlockSpec((1, 1, d), lambda b, t: (b, 0, 0)),
            pl.BlockSpec((1, 1, d), lambda b, t: (b, 0, 1)),
            _resident((d, n)),
            _resident((1, fdim)),
            _resident((1, vdim)),
        ],
        out_specs=pl.BlockSpec((1, tm, heads * vdim), lambda b, t: (b, t, 0)),
        out_shape=jax.ShapeDtypeStruct((bsz, seq, heads * vdim), _BF16),
        scratch_shapes=[pltpu.VMEM((heads, vdim, kdim), _F32)],
        compiler_params=_cparams("parallel", "arbitrary"),
        name="hgrn2_mixer",
    )(x, gain_mix.reshape(1, d), mod, mod, w_in, lb.reshape(1, fdim), g_gain.reshape(1, vdim))


def _gelu_tanh(x):
    return x * (0.5 * (1.0 + jnp.tanh(0.7978845608028654 * (x + 0.044715 * (x * x * x)))))


def _rglru_kernel(x_ref, gn_ref, sh_ref, sc_ref, w_ref, cw_ref, cb_ref, wa_ref, ba_ref, wi_ref, bi_ref,
                  lam_ref, o_ref, hc_ref, xh_ref, *, phases, sub):
    t = pl.program_id(1)

    @pl.when(t == 0)
    def _init():
        hc_ref[...] = jnp.zeros_like(hc_ref)
        xh_ref[...] = jnp.zeros_like(xh_ref)

    width = o_ref.shape[2]
    ng = sub // phases
    shift = ng.bit_length() - 1
    i_r = lax.broadcasted_iota(jnp.int32, (sub, sub), 0)
    i_c = lax.broadcasted_iota(jnp.int32, (sub, sub), 1)
    time_of = lambda i: jnp.bitwise_and(i, ng - 1) * phases + jnp.right_shift(i, shift)
    perm = (i_c == time_of(i_r)).astype(_BF16)
    unperm = (i_r == time_of(i_c)).astype(_BF16)
    slab = lambda val, p: val[p * ng:(p + 1) * ng, :]
    grow = lax.broadcasted_iota(jnp.int32, (ng, 1), 0)
    row = lax.broadcasted_iota(jnp.int32, (sub, 1), 0)
    taps = cw_ref.shape[0]
    nb = wa_ref.shape[0]
    bw = width // nb
    neg_lam = -lam_ref[...]
    softplus = jnp.maximum(neg_lam, 0.0) + jnp.log1p(jnp.exp(-jnp.abs(neg_lam)))

    def project(rows):
        h = _norm_mod(x_ref[0, rows, :], gn_ref[...], sh_ref[0], sc_ref[0]).astype(_BF16)
        h = _dot(perm, h).astype(_BF16)
        return _dot(h, w_ref[:, :width]), _dot(h, w_ref[:, width:])

    tiles = [slice(r, r + sub) for r in range(0, x_ref.shape[1], sub)]
    z = project(tiles[0])
    for i, rows in enumerate(tiles):
        y_pre, x_br = z
        prev_group = {p: jnp.where(grow >= 1, pltpu.roll(slab(x_br, p), 1, 0), xh_ref[p:p + 1, :])
                      for p in range(phases - (taps - 1), phases)}
        for p in range(phases - (taps - 1), phases):
            xh_ref[p:p + 1, :] = slab(x_br, p)[ng - 1:ng, :]
        conv_slabs = []
        for p in range(phases):
            acc = cb_ref[...]
            for j in range(taps):
                src = p - (taps - 1) + j
                acc = acc + (slab(x_br, src) if src >= 0 else prev_group[src + phases]) * cw_ref[j:j + 1, :]
            conv_slabs.append(acc)
        x_conv = jnp.concatenate(conv_slabs, axis=0)
        if i + 1 < len(tiles):
            z = project(tiles[i + 1])

        xcb = x_conv.astype(_BF16)
        r = jnp.concatenate([_dot(xcb[:, n * bw:(n + 1) * bw], wa_ref[n]) for n in range(nb)], axis=1)
        gi = jnp.concatenate([_dot(xcb[:, n * bw:(n + 1) * bw], wi_ref[n]) for n in range(nb)], axis=1)
        r = _sigmoid_tanh(r + ba_ref[...])
        gi = _sigmoid_tanh(gi + bi_ref[...])
        log_a = -RG_C * r * softplus
        a = jnp.exp(log_a)
        th = jnp.tanh(log_a)
        mult = jnp.sqrt(-2.0 * th) * lax.rsqrt(1.0 - th)
        if i == 0:
            mult = jnp.where(jnp.logical_and(row == 0, t == 0), 1.0, mult)
        u = (gi * x_conv) * mult

        h_loc, a_cum = [slab(u, 0)], [slab(a, 0)]
        for p in range(1, phases):
            h_loc.append(slab(a, p) * h_loc[-1] + slab(u, p))
            a_cum.append(slab(a, p) * a_cum[-1])
        ga, gu = a_cum[-1], h_loc[-1]
        d = 1
        while d < ng:
            keep = grow >= d
            a_sh = jnp.where(keep, pltpu.roll(ga, d, 0), 1.0)
            u_sh = jnp.where(keep, pltpu.roll(gu, d, 0), 0.0)
            ga, gu = ga * a_sh, gu + ga * u_sh
            d *= 2
        carry = hc_ref[...]
        h_end = gu + ga * carry
        hc_ref[...] = h_end[ng - 1:ng, :]
        h_in = jnp.where(grow >= 1, pltpu.roll(h_end, 1, 0), carry)
        hs = jnp.concatenate([h_loc[p] + a_cum[p] * h_in for p in range(phases)], axis=0)
        out = (hs * _gelu_tanh(y_pre)).astype(_BF16)
        o_ref[0, rows, :] = _dot(unperm, out).astype(o_ref.dtype)


def _rglru_mixer(x, gain_mix, mod, w_in, conv_w, conv_b, w_a, b_a, w_i, b_i, lam, tm):
    bsz, seq, d = x.shape
    two_w = w_in.shape[1]
    width = two_w // 2
    nb, bw, _ = w_a.shape
    phases = RG_SCAN_PHASES
    sub = min(tm, RG_SUBTILE)
    ng = sub // phases
    assert tm % sub == 0 and sub % phases == 0 and ng % SUBLANES == 0 and ng & (ng - 1) == 0
    assert phases >= RG_CONV_WIDTH
    row = lambda v: v.reshape(1, width)
    return pl.pallas_call(
        functools.partial(_rglru_kernel, phases=phases, sub=sub),
        grid=(bsz, seq // tm),
        in_specs=[
            pl.BlockSpec((1, tm, d), lambda b, t: (b, t, 0)),
            _resident((1, d)),
            pl.BlockSpec((1, 1, d), lambda b, t: (b, 0, 0)),
            pl.BlockSpec((1, 1, d), lambda b, t: (b, 0, 1)),
            _resident((d, two_w)),
            _resident((RG_CONV_WIDTH, width)),
            _resident((1, width)),
            _resident((nb, bw, bw)),
            _resident((1, width)),
            _resident((nb, bw, bw)),
            _resident((1, width)),
            _resident((1, width)),
        ],
        out_specs=pl.BlockSpec((1, tm, width), lambda b, t: (b, t, 0)),
        out_shape=jax.ShapeDtypeStruct((bsz, seq, width), _BF16),
        scratch_shapes=[
            pltpu.VMEM((1, width), _F32),
            pltpu.VMEM((phases, width), _F32),
        ],
        compiler_params=_cparams("parallel", "arbitrary"),
        name="rglru_mixer",
    )(x, gain_mix.reshape(1, d), mod, mod, w_in, conv_w, row(conv_b), w_a.astype(_BF16), row(b_a),
      w_i.astype(_BF16), row(b_i), row(lam))


def _tail_kernel(x_ref, m_ref, wo_ref, g1_ref, gn_ref, sh_ref, sc_ref, g2_ref, wup_ref, wdn_ref,
                 *rest, ff_chunk, final):
    o_ref = rest[-1]
    x1 = x_ref[0] + g1_ref[0] * _dot(m_ref[0], wo_ref[...])
    h = _norm_mod(x1, gn_ref[...], sh_ref[0], sc_ref[0]).astype(_BF16)
    acc = jnp.zeros_like(x1)
    for c in range(0, wup_ref.shape[1], ff_chunk):
        u = jnp.maximum(_dot(h, wup_ref[:, c:c + ff_chunk]), 0.0)
        acc = acc + _dot((u * u).astype(_BF16), wdn_ref[c:c + ff_chunk, :])
    x2 = x1 + g2_ref[0] * acc
    if final:
        x2 = _rms_norm(x2, rest[0][...])
    o_ref[0] = x2


def _layer_tail(x, m, w_o, mod, gain_mlp, w_up, w_down, final_gain, tm):
    bsz, seq, d = x.shape
    dm = m.shape[-1]
    dff = w_up.shape[1]
    final = final_gain is not None
    mod_spec = lambda idx: pl.BlockSpec((1, 1, d), lambda b, t: (b, 0, idx))
    in_specs = [
        pl.BlockSpec((1, tm, d), lambda b, t: (b, t, 0)),
        pl.BlockSpec((1, tm, dm), lambda b, t: (b, t, 0)),
        _resident((dm, d)),
        mod_spec(2),
        _resident((1, d)),
        mod_spec(3),
        mod_spec(4),
        mod_spec(5),
        _resident((d, dff)),
        _resident((dff, d)),
    ]
    args = [x, m, w_o, mod, gain_mlp.reshape(1, d), mod, mod, mod, w_up, w_down]
    if final:
        in_specs.append(_resident((1, d)))
        args.append(final_gain.reshape(1, d))
    return pl.pallas_call(
        functools.partial(_tail_kernel, ff_chunk=min(dff, 1024), final=final),
        grid=(bsz, seq // tm),
        in_specs=in_specs,
        out_specs=pl.BlockSpec((1, tm, d), lambda b, t: (b, t, 0)),
        out_shape=jax.ShapeDtypeStruct((bsz, seq, d), _F32),
        compiler_params=_cparams("parallel", "parallel"),
        name="layer_tail",
    )(*args)


def kernel(x, c, ada_w, ada_b, norm_mix, norm_mlp, mlp_up, mlp_down, moba_wqkv, moba_wo, hgrn_w_in, hgrn_lb, hgrn_norm, hgrn_wo, rg_w_in, rg_conv_w, rg_conv_b, rg_w_a, rg_b_a, rg_w_i, rg_b_i, rg_lambda, rg_wo, final_norm):
    depth = ada_w.shape[0]
    bsz, seq, d = x.shape
    tiles = _tiles(seq)
    bf = lambda w: w.astype(_BF16)

    mod_all = _ada_modulation(c, ada_w, ada_b)
    lb_all = _hgrn_lower_bounds(hgrn_lb)
    i_a = i_b = i_c = 0
    for layer in range(depth):
        mod = mod_all[layer].reshape(bsz, 1, 6 * d)
        kind = layer % N_MIXERS
        if kind == 0:
            qkv = _project(x, norm_mix[layer], mod, bf(moba_wqkv[i_a]), _BF16, tiles["proj"])
            m = _moba_attention(qkv, MOBA_HEADS, tiles["moba_heads"])
            w_o = moba_wo[i_a]
            i_a += 1
        elif kind == 1:
            m = _hgrn_mixer(x, norm_mix[layer], mod, bf(hgrn_w_in[i_b]), lb_all[layer], hgrn_norm[i_b],
                            HGRN_HEADS, tiles["hgrn"])
            w_o = hgrn_wo[i_b]
            i_b += 1
        else:
            m = _rglru_mixer(x, norm_mix[layer], mod, bf(rg_w_in[i_c]), rg_conv_w[i_c], rg_conv_b[i_c],
                             rg_w_a[i_c], rg_b_a[i_c], rg_w_i[i_c], rg_b_i[i_c], rg_lambda[i_c],
                             tiles["rglru"])
            w_o = rg_wo[i_c]
            i_c += 1
        x = _layer_tail(x, m, bf(w_o), mod, norm_mlp[layer], bf(mlp_up[layer]), bf(mlp_down[layer]),
                        final_norm if layer == depth - 1 else None, tiles["tail"])
    return x
```

```python
import functools

import jax
import jax.numpy as jnp
from jax import lax
from jax.experimental import pallas as pl
from jax.experimental.pallas import tpu as pltpu

_F32 = jnp.float32
_BF16 = jnp.bfloat16
_HIGHEST = lax.Precision.HIGHEST

NORM_EPS = 1e-6
N_MIXERS = 3
MOBA_HEADS = 8
MOBA_BLOCK = 256
MOBA_TOPK = 3
HGRN_HEADS = 8
HGRN_CHUNK = 64
HGRN_SUBTILE = 256
RG_BLOCKS = 4
RG_CONV_WIDTH = 4
RG_C = 8.0
RG_SUBTILE = 256
RG_SCAN_PHASES = 8
MASK_VALUE = -1e30

V7X_VMEM_LIMIT_BYTES = 56 * 1024 * 1024
LANES = 128
SUBLANES = 8


def _tiles(seq):
    return dict(proj=min(seq, 1024), tail=min(seq, 1024), hgrn=min(seq, 1024), rglru=min(seq, 1024), moba_heads=2)


def _cparams(*sem):
    return pltpu.CompilerParams(dimension_semantics=sem, vmem_limit_bytes=V7X_VMEM_LIMIT_BYTES)


def _resident(shape):
    zeros = (0,) * len(shape)
    return pl.BlockSpec(shape, lambda *_: zeros, pipeline_mode=pl.Buffered(1))


def _sigmoid(x):
    return 1.0 / (1.0 + jnp.exp(-x))


def _sigmoid_tanh(x):
    return 0.5 * jnp.tanh(0.5 * x) + 0.5


def _rms_norm(x, gain):
    return x * lax.rsqrt(jnp.mean(x * x, axis=-1, keepdims=True) + NORM_EPS) * gain


def _norm_mod(x, gain, shift, scale):
    return _rms_norm(x, gain) * (1.0 + scale) + shift


def _dot(a, b):
    return jnp.dot(a, b, preferred_element_type=_F32)


def _dot_nt(a, b):
    return lax.dot_general(a, b, (((1,), (1,)), ((), ())), preferred_element_type=_F32)


def _dot_tn(a, b):
    return lax.dot_general(a, b, (((0,), (0,)), ((), ())), preferred_element_type=_F32)


def _ada_kernel(c_ref, w_ref, b_ref, o_ref):
    c = c_ref[...]
    cond = c * _sigmoid(c)
    o_ref[0] = jnp.dot(cond, w_ref[0], preferred_element_type=_F32, precision=_HIGHEST) + b_ref[0]


def _ada_modulation(c, ada_w, ada_b):
    depth, d, n = ada_w.shape
    bsz = c.shape[0]
    tn = min(n, 1536)
    return pl.pallas_call(
        _ada_kernel,
        grid=(depth, n // tn),
        in_specs=[
            pl.BlockSpec((bsz, d), lambda l, j: (0, 0)),
            pl.BlockSpec((1, d, tn), lambda l, j: (l, 0, j)),
            pl.BlockSpec((1, 1, tn), lambda l, j: (l, 0, j)),
        ],
        out_specs=pl.BlockSpec((1, bsz, tn), lambda l, j: (l, 0, j)),
        out_shape=jax.ShapeDtypeStruct((depth, bsz, n), _F32),
        compiler_params=_cparams("arbitrary", "arbitrary"),
        name="ada_modulation",
    )(c, ada_w, ada_b.reshape(depth, 1, n))


def _lb_kernel(x_ref, o_ref):
    depth = x_ref.shape[0]
    rows = [x_ref[l:l + 1, :] for l in range(depth)]
    mx = functools.reduce(jnp.maximum, rows)
    es = [jnp.exp(r - mx) for r in rows]
    den = functools.reduce(lambda a, b: a + b, es)
    first = es[0] / den
    run = first
    o_ref[0:1, :] = run - first
    for l in range(1, depth):
        run = run + es[l] / den
        o_ref[l:l + 1, :] = run - first


def _hgrn_lower_bounds(hgrn_lb):
    return pl.pallas_call(
        _lb_kernel,
        out_shape=jax.ShapeDtypeStruct(hgrn_lb.shape, _F32),
        name="hgrn_lower_bounds",
    )(hgrn_lb.astype(_F32))


def _proj_kernel(x_ref, g_ref, sh_ref, sc_ref, w_ref, o_ref, *, n_chunk):
    h = _norm_mod(x_ref[0], g_ref[...], sh_ref[0], sc_ref[0]).astype(_BF16)
    for c in range(0, w_ref.shape[1], n_chunk):
        o_ref[0, :, c:c + n_chunk] = _dot(h, w_ref[:, c:c + n_chunk]).astype(o_ref.dtype)


def _project(x, gain, mod, w, out_dtype, tm):
    bsz, seq, d = x.shape
    n = w.shape[1]
    return pl.pallas_call(
        functools.partial(_proj_kernel, n_chunk=min(n, 1024)),
        grid=(bsz, seq // tm),
        in_specs=[
            pl.BlockSpec((1, tm, d), lambda b, t: (b, t, 0)),
            _resident((1, d)),
            pl.BlockSpec((1, 1, d), lambda b, t: (b, 0, 0)),
            pl.BlockSpec((1, 1, d), lambda b, t: (b, 0, 1)),
            _resident((d, n)),
        ],
        out_specs=pl.BlockSpec((1, tm, n), lambda b, t: (b, t, 0)),
        out_shape=jax.ShapeDtypeStruct((bsz, seq, n), out_dtype),
        compiler_params=_cparams("parallel", "parallel"),
        name="project",
    )(x, gain.reshape(1, d), mod, mod, w)


def _moba_kernel(q_ref, k_ref, v_ref, o_ref, *, blk, topk, scale):
    dh = LANES
    for hd in range(q_ref.shape[2] // dh):
        cols = slice(hd * dh, (hd + 1) * dh)
        _moba_head(q_ref[0, :, cols], k_ref[0, :, cols], v_ref[0, :, cols], o_ref, cols,
                   blk=blk, topk=topk, scale=scale)


def _moba_head(q, k, v, o_ref, cols, *, blk, topk, scale):
    seq, dh = k.shape
    nblk = seq // blk

    row = lax.broadcasted_iota(jnp.int32, (nblk, seq), 0)
    col = lax.broadcasted_iota(jnp.int32, (nblk, seq), 1)
    fully_past = (row + 1) * blk <= col
    kmean = jnp.sum(k.astype(_F32).reshape(nblk, blk, dh), axis=1) * (1.0 / blk)
    kmean_hi = kmean.astype(_BF16)
    kmean_lo = (kmean - kmean_hi.astype(_F32)).astype(_BF16)
    gate_parts = _dot_nt(jnp.concatenate([kmean_hi, kmean_lo], axis=0), q)
    gate_t = gate_parts[:nblk, :] + gate_parts[nblk:, :]

    bias_rows = []
    for j in range(nblk):
        g_j = gate_t[j:j + 1, :]
        beats = jnp.logical_and(
            fully_past, jnp.logical_or(gate_t > g_j, jnp.logical_and(gate_t == g_j, row < j)))
        rank = jnp.sum(beats.astype(_F32), axis=0, keepdims=True)
        masked = jnp.logical_and(fully_past[j:j + 1, :], rank >= topk)
        bias_rows.append(jnp.where(masked, MASK_VALUE, 0.0))
    bias_t = jnp.concatenate(bias_rows + [jnp.zeros((dh - nblk, seq), _F32)], axis=0)

    key_pos = lax.broadcasted_iota(jnp.int32, (seq, dh), 0)
    lane = lax.broadcasted_iota(jnp.int32, (seq, dh), 1)
    onehot = jnp.logical_and(lane * blk <= key_pos, key_pos < (lane + 1) * blk).astype(_BF16)
    k_aug = jnp.concatenate([k, onehot], axis=1)
    v_aug = jnp.concatenate([v, jnp.ones((seq, dh), _BF16)], axis=1)

    qpos = lax.broadcasted_iota(jnp.int32, (blk, blk), 0)
    kpos = lax.broadcasted_iota(jnp.int32, (blk, blk), 1)
    causal = kpos <= qpos
    exp2_scale = scale * 1.4426950408889634
    order = [0] + list(range(nblk - 1, 0, -1))
    logits = {}
    for i in order:
        lo, hi = i * blk, (i + 1) * blk
        if i == 0:
            bias = jnp.zeros((blk, dh), _BF16)
        else:
            bias = jnp.transpose(bias_t[:, lo:hi]).astype(_BF16)
        q_aug = jnp.concatenate([q[lo:hi, :], bias], axis=1)
        logits[i] = _dot_nt(q_aug, k_aug[:hi, :])
    probs = {}
    for i in order:
        lo, hi = i * blk, (i + 1) * blk
        s = logits[i]
        s_own = jnp.where(causal, s[:, lo:hi], MASK_VALUE)
        s = s_own if i == 0 else jnp.concatenate([s[:, :lo], s_own], axis=1)
        m = jnp.max(s, axis=-1, keepdims=True)
        probs[i] = jnp.exp2((s - m) * exp2_scale).astype(_BF16)
    for i in order:
        lo, hi = i * blk, (i + 1) * blk
        o = _dot(probs[i], v_aug[:hi, :])
        o_ref[0, lo:hi, cols] = (o[:, :dh] * (1.0 / o[:, dh:dh + 1])).astype(o_ref.dtype)


def _moba_attention(qkv, heads, heads_per_step):
    bsz, seq, three_hd = qkv.shape
    dh = three_hd // (3 * heads)
    blk = MOBA_BLOCK
    assert seq % blk == 0 and dh == LANES and heads % heads_per_step == 0
    nblk = seq // blk
    topk = min(MOBA_TOPK, nblk - 1)
    groups = heads // heads_per_step
    width = heads_per_step * dh
    kern = functools.partial(_moba_kernel, blk=blk, topk=topk, scale=dh ** -0.5)
    return pl.pallas_call(
        kern,
        grid=(bsz, groups),
        in_specs=[
            pl.BlockSpec((1, seq, width), lambda b, g: (b, 0, g)),
            pl.BlockSpec((1, seq, width), lambda b, g: (b, 0, groups + g)),
            pl.BlockSpec((1, seq, width), lambda b, g: (b, 0, 2 * groups + g)),
        ],
        out_specs=pl.BlockSpec((1, seq, width), lambda b, g: (b, 0, g)),
        out_shape=jax.ShapeDtypeStruct((bsz, seq, heads * dh), _BF16),
        compiler_params=_cparams("parallel", "parallel"),
        name="moba_attention",
    )(qkv, qkv, qkv)


def _split3(x):
    hi = x.astype(_BF16)
    r1 = x - hi.astype(_F32)
    mid = r1.astype(_BF16)
    lo = (r1 - mid.astype(_F32)).astype(_BF16)
    return hi, mid, lo


def _hgrn_kernel(x_ref, gn_ref, sh_ref, sc_ref, w_ref, lb_ref, gain_ref, o_ref, st_ref, *, heads, chunk, sub):
    @pl.when(pl.program_id(1) == 0)
    def _init():
        st_ref[...] = jnp.zeros_like(st_ref)

    fdim = lb_ref.shape[1]
    hv = heads * gain_ref.shape[1]
    tiles = [slice(r, r + sub) for r in range(0, x_ref.shape[1], sub)]

    def project(rows):
        h = _norm_mod(x_ref[0, rows, :], gn_ref[...], sh_ref[0], sc_ref[0]).astype(_BF16)
        return (_dot(h, w_ref[:, :fdim]), _dot(h, w_ref[:, fdim:2 * fdim]),
                _dot(h, w_ref[:, 2 * fdim:2 * fdim + hv]).astype(_BF16), _dot(h, w_ref[:, 2 * fdim + hv:]))

    z = project(tiles[0])
    for i, rows in enumerate(tiles):
        gates = _hgrn_gates(z[0], z[1], lb_ref[...], chunk=chunk)
        v, zg = z[2], z[3]
        if i + 1 < len(tiles):
            z = project(tiles[i + 1])
        _hgrn_heads(gates, v, zg, gain_ref[...], o_ref, rows, st_ref, heads=heads, chunk=chunk)


def _hgrn_gates(zq, zf, lb, *, chunk):
    tm, fdim = zq.shape
    nck = tm // chunk
    q = zq * _sigmoid_tanh(zq)
    fgate = lb + (1.0 - lb) * _sigmoid(zf)
    k = 1.0 - fgate

    shift = chunk.bit_length() - 1
    r_i = lax.broadcasted_iota(jnp.int32, (tm, tm), 0)
    c_i = lax.broadcasted_iota(jnp.int32, (tm, tm), 1)
    block_causal = jnp.logical_and(jnp.right_shift(r_i, shift) == jnp.right_shift(c_i, shift), c_i <= r_i)
    tri = block_causal.astype(_BF16)
    b = functools.reduce(lambda a, c: a + c, [_dot(tri, part) for part in _split3(jnp.log(fgate))])

    by_chunk = lambda t: t.reshape(nck, chunk, fdim)
    b3, q3, k3 = by_chunk(b), by_chunk(q), by_chunk(k)
    b_mid = b3[:, chunk // 2:chunk // 2 + 1, :]
    b_last = b3[:, chunk - 1:chunk, :]
    flat = lambda t: t.astype(_BF16).reshape(tm, fdim)
    qd = flat(q3 * jnp.exp(b3 - b_mid))
    kd = flat(k3 * jnp.exp(b_mid - b3))
    q_in = flat(q3 * jnp.exp(b3))
    k_out = flat(k3 * jnp.exp(b_last - b3))
    decay = jnp.exp(b_last)
    return qd, kd, q_in, k_out, decay, block_causal


def _hgrn_heads(gates, v, zg, gain, o_ref, rows_out, st_ref, *, heads, chunk):
    qd, kd, q_in, k_out, decay, block_causal = gates
    tm, fdim = qd.shape
    nck = tm // chunk
    kdim = fdim // heads
    vdim = gain.shape[1]
    ksl = [slice(hd * kdim, (hd + 1) * kdim) for hd in range(heads)]
    vsl = [slice(hd * vdim, (hd + 1) * vdim) for hd in range(heads)]
    crow = [slice(c * chunk, (c + 1) * chunk) for c in range(nck)]
    a_all = [jnp.where(block_causal, _dot_nt(qd[:, ks], kd[:, ks]), 0.0).astype(_BF16) for ks in ksl]
    o_all = [_dot(a_all[hd], v[:, vsl[hd]]) for hd in range(heads)]
    upd = [[_dot_tn(v[rows, vsl[hd]], k_out[rows, ksl[hd]]) for rows in crow] for hd in range(heads)]
    for hd in range(heads):
        st = st_ref[hd]
        outs = []
        for c in range(nck):
            outs.append(o_all[hd][crow[c], :] + _dot_nt(q_in[crow[c], ksl[hd]], st.astype(_BF16)))
            st = decay[c, :, ksl[hd]] * st + upd[hd][c]
        st_ref[hd] = st
        o = jnp.concatenate(outs, axis=0)
        zg_h = zg[:, vsl[hd]]
        o_ref[0, rows_out, vsl[hd]] = (_rms_norm(o, gain) * (zg_h * _sigmoid_tanh(zg_h))).astype(o_ref.dtype)


def _hgrn_mixer(x, gain_mix, mod, w_in, lb, g_gain, heads, tm):
    bsz, seq, d = x.shape
    n = w_in.shape[1]
    fdim = lb.shape[-1]
    vdim = g_gain.shape[-1]
    kdim = fdim // heads
    sub = min(tm, HGRN_SUBTILE)
    assert kdim == LANES and vdim == LANES and n == 2 * fdim + 2 * heads * vdim
    assert HGRN_CHUNK & (HGRN_CHUNK - 1) == 0 and sub % HGRN_CHUNK == 0 and tm % sub == 0
    return pl.pallas_call(
        functools.partial(_hgrn_kernel, heads=heads, chunk=HGRN_CHUNK, sub=sub),
        grid=(bsz, seq // tm),
        in_specs=[
            pl.BlockSpec((1, tm, d), lambda b, t: (b, t, 0)),
            _resident((1, d)),
            pl.BlockSpec((1, 1, d), lambda b, t: (b, 0, 0)),
            pl.BlockSpec((1, 1, d), lambda b, t: (b, 0, 1)),
            _resident((d, n)),
            _resident((1, fdim)),
            _resident((1, vdim)),
        ],
        out_specs=pl.BlockSpec((1, tm, heads * vdim), lambda b, t: (b, t, 0)),
        out_shape=jax.ShapeDtypeStruct((bsz, seq, heads * vdim), _BF16),
        scratch_shapes=[pltpu.VMEM((heads, vdim, kdim), _F32)],
        compiler_params=_cparams("parallel", "arbitrary"),
        name="hgrn2_mixer",
    )(x, gain_mix.reshape(1, d), mod, mod, w_in, lb.reshape(1, fdim), g_gain.reshape(1, vdim))


def _gelu_tanh(x):
    return x * (0.5 * (1.0 + jnp.tanh(0.7978845608028654 * (x + 0.044715 * (x * x * x)))))


def _rglru_kernel(x_ref, gn_ref, sh_ref, sc_ref, w_ref, cw_ref, cb_ref, wa_ref, ba_ref, wi_ref, bi_ref,
                  lam_ref, o_ref, hc_ref, xh_ref, *, phases, sub):
    t = pl.program_id(1)

    @pl.when(t == 0)
    def _init():
        hc_ref[...] = jnp.zeros_like(hc_ref)
        xh_ref[...] = jnp.zeros_like(xh_ref)

    width = o_ref.shape[2]
    ng = sub // phases
    shift = ng.bit_length() - 1
    i_r = lax.broadcasted_iota(jnp.int32, (sub, sub), 0)
    i_c = lax.broadcasted_iota(jnp.int32, (sub, sub), 1)
    time_of = lambda i: jnp.bitwise_and(i, ng - 1) * phases + jnp.right_shift(i, shift)
    perm = (i_c == time_of(i_r)).astype(_BF16)
    unperm = (i_r == time_of(i_c)).astype(_BF16)
    slab = lambda val, p: val[p * ng:(p + 1) * ng, :]
    grow = lax.broadcasted_iota(jnp.int32, (ng, 1), 0)
    row = lax.broadcasted_iota(jnp.int32, (sub, 1), 0)
    taps = cw_ref.shape[0]
    nb = wa_ref.shape[0]
    bw = width // nb
    neg_lam = -lam_ref[...]
    softplus = jnp.maximum(neg_lam, 0.0) + jnp.log1p(jnp.exp(-jnp.abs(neg_lam)))

    def project(rows):
        h = _norm_mod(x_ref[0, rows, :], gn_ref[...], sh_ref[0], sc_ref[0]).astype(_BF16)
        h = _dot(perm, h).astype(_BF16)
        return _dot(h, w_ref[:, :width]), _dot(h, w_ref[:, width:])

    tiles = [slice(r, r + sub) for r in range(0, x_ref.shape[1], sub)]
    z = project(tiles[0])
    for i, rows in enumerate(tiles):
        y_pre, x_br = z
        prev_group = {p: jnp.where(grow >= 1, pltpu.roll(slab(x_br, p), 1, 0), xh_ref[p:p + 1, :])
                      for p in range(phases - (taps - 1), phases)}
        for p in range(phases - (taps - 1), phases):
            xh_ref[p:p + 1, :] = slab(x_br, p)[ng - 1:ng, :]
        conv_slabs = []
        for p in range(phases):
            acc = cb_ref[...]
            for j in range(taps):
                src = p - (taps - 1) + j
                acc = acc + (slab(x_br, src) if src >= 0 else prev_group[src + phases]) * cw_ref[j:j + 1, :]
            conv_slabs.append(acc)
        x_conv = jnp.concatenate(conv_slabs, axis=0)
        if i + 1 < len(tiles):
            z = project(tiles[i + 1])

        xcb = x_conv.astype(_BF16)
        r = jnp.concatenate([_dot(xcb[:, n * bw:(n + 1) * bw], wa_ref[n]) for n in range(nb)], axis=1)
        gi = jnp.concatenate([_dot(xcb[:, n * bw:(n + 1) * bw], wi_ref[n]) for n in range(nb)], axis=1)
        r = _sigmoid_tanh(r + ba_ref[...])
        gi = _sigmoid_tanh(gi + bi_ref[...])
        log_a = -RG_C * r * softplus
        a = jnp.exp(log_a)
        th = jnp.tanh(log_a)
        mult = jnp.sqrt(-2.0 * th) * lax.rsqrt(1.0 - th)
        if i == 0:
            mult = jnp.where(jnp.logical_and(row == 0, t == 0), 1.0, mult)
        u = (gi * x_conv) * mult

        h_loc, a_cum = [slab(u, 0)], [slab(a, 0)]
        for p in range(1, phases):
            h_loc.append(slab(a, p) * h_loc[-1] + slab(u, p))
            a_cum.append(slab(a, p) * a_cum[-1])
        ga, gu = a_cum[-1], h_loc[-1]
        d = 1
        while d < ng:
            keep = grow >= d
            a_sh = jnp.where(keep, pltpu.roll(ga, d, 0), 1.0)
            u_sh = jnp.where(keep, pltpu.roll(gu, d, 0), 0.0)
            ga, gu = ga * a_sh, gu + ga * u_sh
            d *= 2
        carry = hc_ref[...]
        h_end = gu + ga * carry
        hc_ref[...] = h_end[ng - 1:ng, :]
        h_in = jnp.where(grow >= 1, pltpu.roll(h_end, 1, 0), carry)
        hs = jnp.concatenate([h_loc[p] + a_cum[p] * h_in for p in range(phases)], axis=0)
        out = (hs * _gelu_tanh(y_pre)).astype(_BF16)
        o_ref[0, rows, :] = _dot(unperm, out).astype(o_ref.dtype)


def _rglru_mixer(x, gain_mix, mod, w_in, conv_w, conv_b, w_a, b_a, w_i, b_i, lam, tm):
    bsz, seq, d = x.shape
    two_w = w_in.shape[1]
    width = two_w // 2
    nb, bw, _ = w_a.shape
    phases = RG_SCAN_PHASES
    sub = min(tm, RG_SUBTILE)
    ng = sub // phases
    assert tm % sub == 0 and sub % phases == 0 and ng % SUBLANES == 0 and ng & (ng - 1) == 0
    assert phases >= RG_CONV_WIDTH
    row = lambda v: v.reshape(1, width)
    return pl.pallas_call(
        functools.partial(_rglru_kernel, phases=phases, sub=sub),
        grid=(bsz, seq // tm),
        in_specs=[
            pl.BlockSpec((1, tm, d), lambda b, t: (b, t, 0)),
            _resident((1, d)),
            pl.BlockSpec((1, 1, d), lambda b, t: (b, 0, 0)),
            pl.BlockSpec((1, 1, d), lambda b, t: (b, 0, 1)),
            _resident((d, two_w)),
            _resident((RG_CONV_WIDTH, width)),
            _resident((1, width)),
            _resident((nb, bw, bw)),
            _resident((1, width)),
            _resident((nb, bw, bw)),
            _resident((1, width)),
            _resident((1, width)),
        ],
        out_specs=pl.BlockSpec((1, tm, width), lambda b, t: (b, t, 0)),
        out_shape=jax.ShapeDtypeStruct((bsz, seq, width), _BF16),
        scratch_shapes=[
            pltpu.VMEM((1, width), _F32),
            pltpu.VMEM((phases, width), _F32),
        ],
        compiler_params=_cparams("parallel", "arbitrary"),
        name="rglru_mixer",
    )(x, gain_mix.reshape(1, d), mod, mod, w_in, conv_w, row(conv_b), w_a.astype(_BF16), row(b_a),
      w_i.astype(_BF16), row(b_i), row(lam))


def _tail_kernel(x_ref, m_ref, wo_ref, g1_ref, gn_ref, sh_ref, sc_ref, g2_ref, wup_ref, wdn_ref,
                 *rest, ff_chunk, final):
    o_ref = rest[-1]
    x1 = x_ref[0] + g1_ref[0] * _dot(m_ref[0], wo_ref[...])
    h = _norm_mod(x1, gn_ref[...], sh_ref[0], sc_ref[0]).astype(_BF16)
    acc = jnp.zeros_like(x1)
    for c in range(0, wup_ref.shape[1], ff_chunk):
        u = jnp.maximum(_dot(h, wup_ref[:, c:c + ff_chunk]), 0.0)
        acc = acc + _dot((u * u).astype(_BF16), wdn_ref[c:c + ff_chunk, :])
    x2 = x1 + g2_ref[0] * acc
    if final:
        x2 = _rms_norm(x2, rest[0][...])
    o_ref[0] = x2


def _layer_tail(x, m, w_o, mod, gain_mlp, w_up, w_down, final_gain, tm):
    bsz, seq, d = x.shape
    dm = m.shape[-1]
    dff = w_up.shape[1]
    final = final_gain is not None
    mod_spec = lambda idx: pl.BlockSpec((1, 1, d), lambda b, t: (b, 0, idx))
    in_specs = [
        pl.BlockSpec((1, tm, d), lambda b, t: (b, t, 0)),
        pl.BlockSpec((1, tm, dm), lambda b, t: (b, t, 0)),
        _resident((dm, d)),
        mod_spec(2),
        _resident((1, d)),
        mod_spec(3),
        mod_spec(4),
        mod_spec(5),
        _resident((d, dff)),
        _resident((dff, d)),
    ]
    args = [x, m, w_o, mod, gain_mlp.reshape(1, d), mod, mod, mod, w_up, w_down]
    if final:
        in_specs.append(_resident((1, d)))
        args.append(final_gain.reshape(1, d))
    return pl.pallas_call(
        functools.partial(_tail_kernel, ff_chunk=min(dff, 1024), final=final),
        grid=(bsz, seq // tm),
        in_specs=in_specs,
        out_specs=pl.BlockSpec((1, tm, d), lambda b, t: (b, t, 0)),
        out_shape=jax.ShapeDtypeStruct((bsz, seq, d), _F32),
        compiler_params=_cparams("parallel", "parallel"),
        name="layer_tail",
    )(*args)


def kernel(x, c, ada_w, ada_b, norm_mix, norm_mlp, mlp_up, mlp_down, moba_wqkv, moba_wo, hgrn_w_in, hgrn_lb, hgrn_norm, hgrn_wo, rg_w_in, rg_conv_w, rg_conv_b, rg_w_a, rg_b_a, rg_w_i, rg_b_i, rg_lambda, rg_wo, final_norm):
    depth = ada_w.shape[0]
    bsz, seq, d = x.shape
    tiles = _tiles(seq)
    bf = lambda w: w.astype(_BF16)

    mod_all = _ada_modulation(c, ada_w, ada_b)
    lb_all = _hgrn_lower_bounds(hgrn_lb)
    i_a = i_b = i_c = 0
    for layer in range(depth):
        mod = mod_all[layer].reshape(bsz, 1, 6 * d)
        kind = layer % N_MIXERS
        if kind == 0:
            qkv = _project(x, norm_mix[layer], mod, bf(moba_wqkv[i_a]), _BF16, tiles["proj"])
            m = _moba_attention(qkv, MOBA_HEADS, tiles["moba_heads"])
            w_o = moba_wo[i_a]
            i_a += 1
        elif kind == 1:
            m = _hgrn_mixer(x, norm_mix[layer], mod, bf(hgrn_w_in[i_b]), lb_all[layer], hgrn_norm[i_b],
                            HGRN_HEADS, tiles["hgrn"])
            w_o = hgrn_wo[i_b]
            i_b += 1
        else:
            m = _rglru_mixer(x, norm_mix[layer], mod, bf(rg_w_in[i_c]), rg_conv_w[i_c], rg_conv_b[i_c],
                             rg_w_a[i_c], rg_b_a[i_c], rg_w_i[i_c], rg_b_i[i_c], rg_lambda[i_c],
                             tiles["rglru"])
            w_o = rg_wo[i_c]
            i_c += 1
        x = _layer_tail(x, m, bf(w_o), mod, norm_mlp[layer], bf(mlp_up[layer]), bf(mlp_down[layer]),
                        final_norm if layer == depth - 1 else None, tiles["tail"])
    return x
```

```python
import functools

import jax
import jax.numpy as jnp
from jax import lax
from jax.experimental import pallas as pl
from jax.experimental.pallas import tpu as pltpu

_F32 = jnp.float32
_BF16 = jnp.bfloat16
_HIGHEST = lax.Precision.HIGHEST

NORM_EPS = 1e-6
N_MIXERS = 3
MOBA_HEADS = 8
MOBA_BLOCK = 256
MOBA_TOPK = 3
HGRN_HEADS = 8
HGRN_CHUNK = 64
HGRN_SUBTILE = 256
RG_BLOCKS = 4
RG_CONV_WIDTH = 4
RG_C = 8.0
RG_SUBTILE = 256
RG_SCAN_PHASES = 8
MASK_VALUE = -1e30

V7X_VMEM_LIMIT_BYTES = 56 * 1024 * 1024
LANES = 128
SUBLANES = 8


def _tiles(seq):
    return dict(proj=min(seq, 1024), tail=min(seq, 1024), hgrn=min(seq, 1024), rglru=min(seq, 1024), moba_heads=4)


def _cparams(*sem):
    return pltpu.CompilerParams(dimension_semantics=sem, vmem_limit_bytes=V7X_VMEM_LIMIT_BYTES)


def _resident(shape):
    zeros = (0,) * len(shape)
    return pl.BlockSpec(shape, lambda *_: zeros, pipeline_mode=pl.Buffered(1))


def _sigmoid(x):
    return 1.0 / (1.0 + jnp.exp(-x))


def _sigmoid_tanh(x):
    return 0.5 * jnp.tanh(0.5 * x) + 0.5


def _rms_norm(x, gain):
    return x * lax.rsqrt(jnp.mean(x * x, axis=-1, keepdims=True) + NORM_EPS) * gain


def _norm_mod(x, gain, shift, scale):
    return _rms_norm(x, gain) * (1.0 + scale) + shift


def _dot(a, b):
    return jnp.dot(a, b, preferred_element_type=_F32)


def _dot_nt(a, b):
    return lax.dot_general(a, b, (((1,), (1,)), ((), ())), preferred_element_type=_F32)


def _ada_kernel(c_ref, w_ref, b_ref, o_ref):
    c = c_ref[...]
    cond = c * _sigmoid(c)
    o_ref[0] = jnp.dot(cond, w_ref[0], preferred_element_type=_F32, precision=_HIGHEST) + b_ref[0]


def _ada_modulation(c, ada_w, ada_b):
    depth, d, n = ada_w.shape
    bsz = c.shape[0]
    tn = min(n, 1536)
    return pl.pallas_call(
        _ada_kernel,
        grid=(depth, n // tn),
        in_specs=[
            pl.BlockSpec((bsz, d), lambda l, j: (0, 0)),
            pl.BlockSpec((1, d, tn), lambda l, j: (l, 0, j)),
            pl.BlockSpec((1, 1, tn), lambda l, j: (l, 0, j)),
        ],
        out_specs=pl.BlockSpec((1, bsz, tn), lambda l, j: (l, 0, j)),
        out_shape=jax.ShapeDtypeStruct((depth, bsz, n), _F32),
        compiler_params=_cparams("arbitrary", "arbitrary"),
        name="ada_modulation",
    )(c, ada_w, ada_b.reshape(depth, 1, n))


def _lb_kernel(x_ref, o_ref):
    depth = x_ref.shape[0]
    rows = [x_ref[l:l + 1, :] for l in range(depth)]
    mx = functools.reduce(jnp.maximum, rows)
    es = [jnp.exp(r - mx) for r in rows]
    den = functools.reduce(lambda a, b: a + b, es)
    first = es[0] / den
    run = first
    o_ref[0:1, :] = run - first
    for l in range(1, depth):
        run = run + es[l] / den
        o_ref[l:l + 1, :] = run - first


def _hgrn_lower_bounds(hgrn_lb):
    return pl.pallas_call(
        _lb_kernel,
        out_shape=jax.ShapeDtypeStruct(hgrn_lb.shape, _F32),
        name="hgrn_lower_bounds",
    )(hgrn_lb.astype(_F32))


def _proj_kernel(x_ref, g_ref, sh_ref, sc_ref, w_ref, o_ref, *, n_chunk):
    h = _norm_mod(x_ref[0], g_ref[...], sh_ref[0], sc_ref[0]).astype(_BF16)
    for c in range(0, w_ref.shape[1], n_chunk):
        o_ref[0, :, c:c + n_chunk] = _dot(h, w_ref[:, c:c + n_chunk]).astype(o_ref.dtype)


def _project(x, gain, mod, w, out_dtype, tm):
    bsz, seq, d = x.shape
    n = w.shape[1]
    return pl.pallas_call(
        functools.partial(_proj_kernel, n_chunk=min(n, 1024)),
        grid=(bsz, seq // tm),
        in_specs=[
            pl.BlockSpec((1, tm, d), lambda b, t: (b, t, 0)),
            _resident((1, d)),
            pl.BlockSpec((1, 1, d), lambda b, t: (b, 0, 0)),
            pl.BlockSpec((1, 1, d), lambda b, t: (b, 0, 1)),
            _resident((d, n)),
        ],
        out_specs=pl.BlockSpec((1, tm, n), lambda b, t: (b, t, 0)),
        out_shape=jax.ShapeDtypeStruct((bsz, seq, n), out_dtype),
        compiler_params=_cparams("parallel", "parallel"),
        name="project",
    )(x, gain.reshape(1, d), mod, mod, w)


def _moba_kernel(q_ref, k_ref, v_ref, o_ref, *, blk, topk, scale):
    dh = LANES
    for hd in range(q_ref.shape[2] // dh):
        cols = slice(hd * dh, (hd + 1) * dh)
        _moba_head(q_ref[0, :, cols], k_ref[0, :, cols], v_ref[0, :, cols], o_ref, cols,
                   blk=blk, topk=topk, scale=scale)


def _moba_head(q, k, v, o_ref, cols, *, blk, topk, scale):
    seq, dh = k.shape
    nblk = seq // blk

    row = lax.broadcasted_iota(jnp.int32, (nblk, seq), 0)
    col = lax.broadcasted_iota(jnp.int32, (nblk, seq), 1)
    fully_past = (row + 1) * blk <= col
    kmean = jnp.sum(k.astype(_F32).reshape(nblk, blk, dh), axis=1) * (1.0 / blk)
    kmean_hi = kmean.astype(_BF16)
    kmean_lo = (kmean - kmean_hi.astype(_F32)).astype(_BF16)
    gate_parts = _dot_nt(jnp.concatenate([kmean_hi, kmean_lo], axis=0), q)
    gate_t = gate_parts[:nblk, :] + gate_parts[nblk:, :]

    bias_rows = []
    for j in range(nblk):
        g_j = gate_t[j:j + 1, :]
        beats = jnp.logical_and(
            fully_past, jnp.logical_or(gate_t > g_j, jnp.logical_and(gate_t == g_j, row < j)))
        rank = jnp.sum(beats.astype(_F32), axis=0, keepdims=True)
        masked = jnp.logical_and(fully_past[j:j + 1, :], rank >= topk)
        bias_rows.append(jnp.where(masked, MASK_VALUE, 0.0))
    bias_t = jnp.concatenate(bias_rows + [jnp.zeros((dh - nblk, seq), _F32)], axis=0)

    key_pos = lax.broadcasted_iota(jnp.int32, (seq, dh), 0)
    lane = lax.broadcasted_iota(jnp.int32, (seq, dh), 1)
    onehot = jnp.logical_and(lane * blk <= key_pos, key_pos < (lane + 1) * blk).astype(_BF16)
    k_aug = jnp.concatenate([k, onehot], axis=1)
    v_aug = jnp.concatenate([v, jnp.ones((seq, dh), _BF16)], axis=1)

    qpos = lax.broadcasted_iota(jnp.int32, (blk, blk), 0)
    kpos = lax.broadcasted_iota(jnp.int32, (blk, blk), 1)
    causal = kpos <= qpos
    exp2_scale = scale * 1.4426950408889634
    order = [0] + list(range(nblk - 1, 0, -1))
    logits = {}
    for i in order:
        lo, hi = i * blk, (i + 1) * blk
        if i == 0:
            bias = jnp.zeros((blk, dh), _BF16)
        else:
            bias = jnp.transpose(bias_t[:, lo:hi]).astype(_BF16)
        q_aug = jnp.concatenate([q[lo:hi, :], bias], axis=1)
        logits[i] = _dot_nt(q_aug, k_aug[:hi, :])
    probs = {}
    for i in order:
        lo, hi = i * blk, (i + 1) * blk
        s = logits[i]
        s_own = jnp.where(causal, s[:, lo:hi], MASK_VALUE)
        s = s_own if i == 0 else jnp.concatenate([s[:, :lo], s_own], axis=1)
        m = jnp.max(s, axis=-1, keepdims=True)
        probs[i] = jnp.exp2((s - m) * exp2_scale).astype(_BF16)
    for i in order:
        lo, hi = i * blk, (i + 1) * blk
        o = _dot(probs[i], v_aug[:hi, :])
        o_ref[0, lo:hi, cols] = (o[:, :dh] * (1.0 / o[:, dh:dh + 1])).astype(o_ref.dtype)


def _moba_attention(qkv, heads, heads_per_step):
    bsz, seq, three_hd = qkv.shape
    dh = three_hd // (3 * heads)
    blk = MOBA_BLOCK
    assert seq % blk == 0 and dh == LANES and heads % heads_per_step == 0
    nblk = seq // blk
    topk = min(MOBA_TOPK, nblk - 1)
    groups = heads // heads_per_step
    width = heads_per_step * dh
    kern = functools.partial(_moba_kernel, blk=blk, topk=topk, scale=dh ** -0.5)
    return pl.pallas_call(
        kern,
        grid=(bsz, groups),
        in_specs=[
            pl.BlockSpec((1, seq, width), lambda b, g: (b, 0, g)),
            pl.BlockSpec((1, seq, width), lambda b, g: (b, 0, groups + g)),
            pl.BlockSpec((1, seq, width), lambda b, g: (b, 0, 2 * groups + g)),
        ],
        out_specs=pl.BlockSpec((1, seq, width), lambda b, g: (b, 0, g)),
        out_shape=jax.ShapeDtypeStruct((bsz, seq, heads * dh), _BF16),
        compiler_params=_cparams("parallel", "parallel"),
        name="moba_attention",
    )(qkv, qkv, qkv)


def _split3(x):
    hi = x.astype(_BF16)
    r1 = x - hi.astype(_F32)
    mid = r1.astype(_BF16)
    lo = (r1 - mid.astype(_F32)).astype(_BF16)
    return hi, mid, lo


def _hgrn_kernel(x_ref, gn_ref, sh_ref, sc_ref, w_ref, lb_ref, gain_ref, o_ref, st_ref, *, heads, chunk, sub):
    @pl.when(pl.program_id(1) == 0)
    def _init():
        st_ref[...] = jnp.zeros_like(st_ref)

    fdim = lb_ref.shape[1]
    hv = heads * gain_ref.shape[1]
    tiles = [slice(r, r + sub) for r in range(0, x_ref.shape[1], sub)]

    def project(rows):
        h = _norm_mod(x_ref[0, rows, :], gn_ref[...], sh_ref[0], sc_ref[0]).astype(_BF16)
        return (_dot(h, w_ref[:, :fdim]), _dot(h, w_ref[:, fdim:2 * fdim]),
                _dot(h, w_ref[:, 2 * fdim:2 * fdim + hv]).astype(_BF16), _dot(h, w_ref[:, 2 * fdim + hv:]))

    z = project(tiles[0])
    for i, rows in enumerate(tiles):
        gates = _hgrn_gates(z[0], z[1], lb_ref[...], chunk=chunk)
        v, zg = z[2], z[3]
        if i + 1 < len(tiles):
            z = project(tiles[i + 1])
        _hgrn_heads(gates, v, zg, gain_ref[...], o_ref, rows, st_ref, heads=heads, chunk=chunk)


def _hgrn_gates(zq, zf, lb, *, chunk):
    tm, fdim = zq.shape
    nck = tm // chunk
    q = zq * _sigmoid_tanh(zq)
    fgate = lb + (1.0 - lb) * _sigmoid(zf)
    k = 1.0 - fgate

    shift = chunk.bit_length() - 1
    r_i = lax.broadcasted_iota(jnp.int32, (tm, tm), 0)
    c_i = lax.broadcasted_iota(jnp.int32, (tm, tm), 1)
    block_causal = jnp.logical_and(jnp.right_shift(r_i, shift) == jnp.right_shift(c_i, shift), c_i <= r_i)
    tri = block_causal.astype(_BF16)
    b = functools.reduce(lambda a, c: a + c, [_dot(tri, part) for part in _split3(jnp.log(fgate))])

    by_chunk = lambda t: t.reshape(nck, chunk, fdim)
    b3, q3, k3 = by_chunk(b), by_chunk(q), by_chunk(k)
    b_mid = b3[:, chunk // 2:chunk // 2 + 1, :]
    b_last = b3[:, chunk - 1:chunk, :]
    flat = lambda t: t.astype(_BF16).reshape(tm, fdim)
    qd = flat(q3 * jnp.exp(b3 - b_mid))
    kd = flat(k3 * jnp.exp(b_mid - b3))
    q_in = flat(q3 * jnp.exp(b3))
    k_out = flat(k3 * jnp.exp(b_last - b3))
    decay = jnp.exp(b_last)
    return qd, kd, q_in, k_out, decay, block_causal


def _hgrn_heads(gates, v, zg, gain, o_ref, rows_out, st_ref, *, heads, chunk):
    qd, kd, q_in, k_out, decay, block_causal = gates
    tm, fdim = qd.shape
    nck = tm // chunk
    kdim = fdim // heads
    vdim = gain.shape[1]
    ksl = [slice(hd * kdim, (hd + 1) * kdim) for hd in range(heads)]
    vsl = [slice(hd * vdim, (hd + 1) * vdim) for hd in range(heads)]
    crow = [slice(c * chunk, (c + 1) * chunk) for c in range(nck)]
    causal = block_causal[:chunk, :chunk]
    a_all = [[jnp.where(causal, _dot_nt(qd[rows, ks], kd[rows, ks]), 0.0).astype(_BF16) for rows in crow]
             for ks in ksl]
    v_t = [[jnp.transpose(v[rows, vs].astype(_F32)).astype(_BF16) for rows in crow] for vs in vsl]
    upd = [[_dot(v_t[hd][c], k_out[crow[c], ksl[hd]]) for c in range(nck)] for hd in range(heads)]
    for hd in range(heads):
        st = st_ref[hd]
        outs = []
        for c in range(nck):
            lhs = jnp.concatenate([q_in[crow[c], ksl[hd]], a_all[hd][c]], axis=1)
            rhs_t = jnp.concatenate([st.astype(_BF16), v_t[hd][c]], axis=1)
            outs.append(_dot_nt(lhs, rhs_t))
            st = decay[c, :, ksl[hd]] * st + upd[hd][c]
        st_ref[hd] = st
        o = jnp.concatenate(outs, axis=0)
        zg_h = zg[:, vsl[hd]]
        o_ref[0, rows_out, vsl[hd]] = (_rms_norm(o, gain) * (zg_h * _sigmoid_tanh(zg_h))).astype(o_ref.dtype)


def _hgrn_mixer(x, gain_mix, mod, w_in, lb, g_gain, heads, tm):
    bsz, seq, d = x.shape
    n = w_in.shape[1]
    fdim = lb.shape[-1]
    vdim = g_gain.shape[-1]
    kdim = fdim // heads
    sub = min(tm, HGRN_SUBTILE)
    assert kdim == LANES and vdim == LANES and n == 2 * fdim + 2 * heads * vdim
    assert HGRN_CHUNK & (HGRN_CHUNK - 1) == 0 and sub % HGRN_CHUNK == 0 and tm % sub == 0
    return pl.pallas_call(
        functools.partial(_hgrn_kernel, heads=heads, chunk=HGRN_CHUNK, sub=sub),
        grid=(bsz, seq // tm),
        in_specs=[
            pl.BlockSpec((1, tm, d), lambda b, t: (b, t, 0)),
            _resident((1, d)),
            pl.BlockSpec((1, 1, d), lambda b, t: (b, 0, 0)),
            pl.BlockSpec((1, 1, d), lambda b, t: (b, 0, 1)),
            _resident((d, n)),
            _resident((1, fdim)),
            _resident((1, vdim)),
        ],
        out_specs=pl.BlockSpec((1, tm, heads * vdim), lambda b, t: (b, t, 0)),
        out_shape=jax.ShapeDtypeStruct((bsz, seq, heads * vdim), _BF16),
        scratch_shapes=[pltpu.VMEM((heads, vdim, kdim), _F32)],
        compiler_params=_cparams("parallel", "arbitrary"),
        name="hgrn2_mixer",
    )(x, gain_mix.reshape(1, d), mod, mod, w_in, lb.reshape(1, fdim), g_gain.reshape(1, vdim))


def _gelu_tanh(x):
    return x * (0.5 * (1.0 + jnp.tanh(0.7978845608028654 * (x + 0.044715 * (x * x * x)))))


def _rglru_kernel(x_ref, gn_ref, sh_ref, sc_ref, w_ref, cw_ref, cb_ref, wa_ref, ba_ref, wi_ref, bi_ref,
                  lam_ref, o_ref, hc_ref, xh_ref, *, phases, sub):
    t = pl.program_id(1)

    @pl.when(t == 0)
    def _init():
        hc_ref[...] = jnp.zeros_like(hc_ref)
        xh_ref[...] = jnp.zeros_like(xh_ref)

    width = o_ref.shape[2]
    ng = sub // phases
    shift = ng.bit_length() - 1
    i_r = lax.broadcasted_iota(jnp.int32, (sub, sub), 0)
    i_c = lax.broadcasted_iota(jnp.int32, (sub, sub), 1)
    time_of = lambda i: jnp.bitwise_and(i, ng - 1) * phases + jnp.right_shift(i, shift)
    perm = (i_c == time_of(i_r)).astype(_BF16)
    unperm = (i_r == time_of(i_c)).astype(_BF16)
    slab = lambda val, p: val[p * ng:(p + 1) * ng, :]
    grow = lax.broadcasted_iota(jnp.int32, (ng, 1), 0)
    row = lax.broadcasted_iota(jnp.int32, (sub, 1), 0)
    taps = cw_ref.shape[0]
    nb = wa_ref.shape[0]
    bw = width // nb
    neg_lam = -lam_ref[...]
    softplus = jnp.maximum(neg_lam, 0.0) + jnp.log1p(jnp.exp(-jnp.abs(neg_lam)))

    def project(rows):
        h = _norm_mod(x_ref[0, rows, :], gn_ref[...], sh_ref[0], sc_ref[0]).astype(_BF16)
        h = _dot(perm, h).astype(_BF16)
        return _dot(h, w_ref[:, :width]), _dot(h, w_ref[:, width:])

    tiles = [slice(r, r + sub) for r in range(0, x_ref.shape[1], sub)]
    z = project(tiles[0])
    for i, rows in enumerate(tiles):
        y_pre, x_br = z
        prev_group = {p: jnp.where(grow >= 1, pltpu.roll(slab(x_br, p), 1, 0), xh_ref[p:p + 1, :])
                      for p in range(phases - (taps - 1), phases)}
        for p in range(phases - (taps - 1), phases):
            xh_ref[p:p + 1, :] = slab(x_br, p)[ng - 1:ng, :]
        conv_slabs = []
        for p in range(phases):
            acc = cb_ref[...]
            for j in range(taps):
                src = p - (taps - 1) + j
                acc = acc + (slab(x_br, src) if src >= 0 else prev_group[src + phases]) * cw_ref[j:j + 1, :]
            conv_slabs.append(acc)
        x_conv = jnp.concatenate(conv_slabs, axis=0)
        if i + 1 < len(tiles):
            z = project(tiles[i + 1])

        xcb = x_conv.astype(_BF16)
        r = jnp.concatenate([_dot(xcb[:, n * bw:(n + 1) * bw], wa_ref[n]) for n in range(nb)], axis=1)
        gi = jnp.concatenate([_dot(xcb[:, n * bw:(n + 1) * bw], wi_ref[n]) for n in range(nb)], axis=1)
        r = _sigmoid_tanh(r + ba_ref[...])
        gi = _sigmoid_tanh(gi + bi_ref[...])
        log_a = -RG_C * r * softplus
        a = jnp.exp(log_a)
        th = jnp.tanh(log_a)
        mult = jnp.sqrt(-2.0 * th) * lax.rsqrt(1.0 - th)
        if i == 0:
            mult = jnp.where(jnp.logical_and(row == 0, t == 0), 1.0, mult)
        u = (gi * x_conv) * mult

        h_loc, a_cum = [slab(u, 0)], [slab(a, 0)]
        for p in range(1, phases):
            h_loc.append(slab(a, p) * h_loc[-1] + slab(u, p))
            a_cum.append(slab(a, p) * a_cum[-1])
        ga, gu = a_cum[-1], h_loc[-1]
        d = 1
        while d < ng:
            keep = grow >= d
            a_sh = jnp.where(keep, pltpu.roll(ga, d, 0), 1.0)
            u_sh = jnp.where(keep, pltpu.roll(gu, d, 0), 0.0)
            ga, gu = ga * a_sh, gu + ga * u_sh
            d *= 2
        carry = hc_ref[...]
        h_end = gu + ga * carry
        hc_ref[...] = h_end[ng - 1:ng, :]
        h_in = jnp.where(grow >= 1, pltpu.roll(h_end, 1, 0), carry)
        hs = jnp.concatenate([h_loc[p] + a_cum[p] * h_in for p in range(phases)], axis=0)
        out = (hs * _gelu_tanh(y_pre)).astype(_BF16)
        o_ref[0, rows, :] = _dot(unperm, out).astype(o_ref.dtype)


def _rglru_mixer(x, gain_mix, mod, w_in, conv_w, conv_b, w_a, b_a, w_i, b_i, lam, tm):
    bsz, seq, d = x.shape
    two_w = w_in.shape[1]
    width = two_w // 2
    nb, bw, _ = w_a.shape
    phases = RG_SCAN_PHASES
    sub = min(tm, RG_SUBTILE)
    ng = sub // phases
    assert tm % sub == 0 and sub % phases == 0 and ng % SUBLANES == 0 and ng & (ng - 1) == 0
    assert phases >= RG_CONV_WIDTH
    row = lambda v: v.reshape(1, width)
    return pl.pallas_call(
        functools.partial(_rglru_kernel, phases=phases, sub=sub),
        grid=(bsz, seq // tm),
        in_specs=[
            pl.BlockSpec((1, tm, d), lambda b, t: (b, t, 0)),
            _resident((1, d)),
            pl.BlockSpec((1, 1, d), lambda b, t: (b, 0, 0)),
            pl.BlockSpec((1, 1, d), lambda b, t: (b, 0, 1)),
            _resident((d, two_w)),
            _resident((RG_CONV_WIDTH, width)),
            _resident((1, width)),
            _resident((nb, bw, bw)),
            _resident((1, width)),
            _resident((nb, bw, bw)),
            _resident((1, width)),
            _resident((1, width)),
        ],
        out_specs=pl.BlockSpec((1, tm, width), lambda b, t: (b, t, 0)),
        out_shape=jax.ShapeDtypeStruct((bsz, seq, width), _BF16),
        scratch_shapes=[
            pltpu.VMEM((1, width), _F32),
            pltpu.VMEM((phases, width), _F32),
        ],
        compiler_params=_cparams("parallel", "arbitrary"),
        name="rglru_mixer",
    )(x, gain_mix.reshape(1, d), mod, mod, w_in, conv_w, row(conv_b), w_a.astype(_BF16), row(b_a),
      w_i.astype(_BF16), row(b_i), row(lam))


def _tail_kernel(x_ref, m_ref, wo_ref, g1_ref, gn_ref, sh_ref, sc_ref, g2_ref, wup_ref, wdn_ref,
                 *rest, ff_chunk, final):
    o_ref = rest[-1]
    x1 = x_ref[0] + g1_ref[0] * _dot(m_ref[0], wo_ref[...])
    h = _norm_mod(x1, gn_ref[...], sh_ref[0], sc_ref[0]).astype(_BF16)
    acc = jnp.zeros_like(x1)
    for c in range(0, wup_ref.shape[1], ff_chunk):
        u = jnp.maximum(_dot(h, wup_ref[:, c:c + ff_chunk]), 0.0)
        acc = acc + _dot((u * u).astype(_BF16), wdn_ref[c:c + ff_chunk, :])
    x2 = x1 + g2_ref[0] * acc
    if final:
        x2 = _rms_norm(x2, rest[0][...])
    o_ref[0] = x2


def _layer_tail(x, m, w_o, mod, gain_mlp, w_up, w_down, final_gain, tm):
    bsz, seq, d = x.shape
    dm = m.shape[-1]
    dff = w_up.shape[1]
    final = final_gain is not None
    mod_spec = lambda idx: pl.BlockSpec((1, 1, d), lambda b, t: (b, 0, idx))
    in_specs = [
        pl.BlockSpec((1, tm, d), lambda b, t: (b, t, 0)),
        pl.BlockSpec((1, tm, dm), lambda b, t: (b, t, 0)),
        _resident((dm, d)),
        mod_spec(2),
        _resident((1, d)),
        mod_spec(3),
        mod_spec(4),
        mod_spec(5),
        _resident((d, dff)),
        _resident((dff, d)),
    ]
    args = [x, m, w_o, mod, gain_mlp.reshape(1, d), mod, mod, mod, w_up, w_down]
    if final:
        in_specs.append(_resident((1, d)))
        args.append(final_gain.reshape(1, d))
    return pl.pallas_call(
        functools.partial(_tail_kernel, ff_chunk=min(dff, 1024), final=final),
        grid=(bsz, seq // tm),
        in_specs=in_specs,
        out_specs=pl.BlockSpec((1, tm, d), lambda b, t: (b, t, 0)),
        out_shape=jax.ShapeDtypeStruct((bsz, seq, d), _F32),
        compiler_params=_cparams("parallel", "parallel"),
        name="layer_tail",
    )(*args)


def kernel(x, c, ada_w, ada_b, norm_mix, norm_mlp, mlp_up, mlp_down, moba_wqkv, moba_wo, hgrn_w_in, hgrn_lb, hgrn_norm, hgrn_wo, rg_w_in, rg_conv_w, rg_conv_b, rg_w_a, rg_b_a, rg_w_i, rg_b_i, rg_lambda, rg_wo, final_norm):
    depth = ada_w.shape[0]
    bsz, seq, d = x.shape
    tiles = _tiles(seq)
    bf = lambda w: w.astype(_BF16)

    mod_all = _ada_modulation(c, ada_w, ada_b)
    lb_all = _hgrn_lower_bounds(hgrn_lb)
    i_a = i_b = i_c = 0
    for layer in range(depth):
        mod = mod_all[layer].reshape(bsz, 1, 6 * d)
        kind = layer % N_MIXERS
        if kind == 0:
            qkv = _project(x, norm_mix[layer], mod, bf(moba_wqkv[i_a]), _BF16, tiles["proj"])
            m = _moba_attention(qkv, MOBA_HEADS, tiles["moba_heads"])
            w_o = moba_wo[i_a]
            i_a += 1
        elif kind == 1:
            m = _hgrn_mixer(x, norm_mix[layer], mod, bf(hgrn_w_in[i_b]), lb_all[layer], hgrn_norm[i_b],
                            HGRN_HEADS, tiles["hgrn"])
            w_o = hgrn_wo[i_b]
            i_b += 1
        else:
            m = _rglru_mixer(x, norm_mix[layer], mod, bf(rg_w_in[i_c]), rg_conv_w[i_c], rg_conv_b[i_c],
                             rg_w_a[i_c], rg_b_a[i_c], rg_w_i[i_c], rg_b_i[i_c], rg_lambda[i_c],
                             tiles["rglru"])
            w_o = rg_wo[i_c]
            i_c += 1
        x = _layer_tail(x, m, bf(w_o), mod, norm_mlp[layer], bf(mlp_up[layer]), bf(mlp_down[layer]),
                        final_norm if layer == depth - 1 else None, tiles["tail"])
    return x
```

```python
import functools

import jax
import jax.numpy as jnp
from jax import lax
from jax.experimental import pallas as pl
from jax.experimental.pallas import tpu as pltpu

_F32 = jnp.float32
_BF16 = jnp.bfloat16

NORM_EPS = 1e-6
N_MIXERS = 3
MOBA_HEADS = 8
MOBA_BLOCK = 256
MOBA_TOPK = 3
HGRN_HEADS = 8
HGRN_CHUNK = 64
HGRN_SUBTILE = 256
RG_BLOCKS = 4
RG_CONV_WIDTH = 4
RG_C = 8.0
RG_SUBTILE = 256
RG_SCAN_PHASES = 8
MASK_VALUE = -1e30

V7X_VMEM_LIMIT_BYTES = 56 * 1024 * 1024
LANES = 128
SUBLANES = 8


def _tiles(seq):
    return dict(proj=min(seq, 1024), tail=min(seq, 1024), hgrn=min(seq, 1024), rglru=min(seq, 1024), moba_heads=4)


def _cparams(*sem):
    return pltpu.CompilerParams(dimension_semantics=sem, vmem_limit_bytes=V7X_VMEM_LIMIT_BYTES)


def _resident(shape):
    zeros = (0,) * len(shape)
    return pl.BlockSpec(shape, lambda *_: zeros, pipeline_mode=pl.Buffered(1))


def _sigmoid(x):
    return 1.0 / (1.0 + jnp.exp(-x))


def _sigmoid_tanh(x):
    return 0.5 * jnp.tanh(0.5 * x) + 0.5


def _silu_tanh(x):
    half = 0.5 * x
    return half + half * jnp.tanh(half)


def _rms_norm(x, gain):
    return x * lax.rsqrt(jnp.mean(x * x, axis=-1, keepdims=True) + NORM_EPS) * gain


def _norm_mod(x, gain, shift, scale):
    return x * lax.rsqrt(jnp.mean(x * x, axis=-1, keepdims=True) + NORM_EPS) * (gain * (1.0 + scale)) + shift


def _dot(a, b):
    return jnp.dot(a, b, preferred_element_type=_F32)


def _dot_nt(a, b):
    return lax.dot_general(a, b, (((1,), (1,)), ((), ())), preferred_element_type=_F32)


def _ada_kernel(c_ref, w_ref, b_ref, o_ref):
    c = c_ref[...]
    cond = c * _sigmoid(c)
    bsz = cond.shape[0]
    w = w_ref[0]
    c_hi = cond.astype(_BF16)
    c_lo = (cond - c_hi.astype(_F32)).astype(_BF16)
    w_hi = w.astype(_BF16)
    w_lo = (w - w_hi.astype(_F32)).astype(_BF16)
    with_hi = _dot(jnp.concatenate([c_hi, c_lo], axis=0), w_hi)
    o_ref[0] = with_hi[:bsz] + with_hi[bsz:] + _dot(c_hi, w_lo) + b_ref[0]


def _ada_modulation(c, ada_w, ada_b):
    depth, d, n = ada_w.shape
    bsz = c.shape[0]
    tn = min(n, 1536)
    return pl.pallas_call(
        _ada_kernel,
        grid=(depth, n // tn),
        in_specs=[
            pl.BlockSpec((bsz, d), lambda l, j: (0, 0)),
            pl.BlockSpec((1, d, tn), lambda l, j: (l, 0, j)),
            pl.BlockSpec((1, 1, tn), lambda l, j: (l, 0, j)),
        ],
        out_specs=pl.BlockSpec((1, bsz, tn), lambda l, j: (l, 0, j)),
        out_shape=jax.ShapeDtypeStruct((depth, bsz, n), _F32),
        compiler_params=_cparams("arbitrary", "arbitrary"),
        name="ada_modulation",
    )(c, ada_w, ada_b.reshape(depth, 1, n))


def _lb_kernel(x_ref, o_ref):
    depth = x_ref.shape[0]
    rows = [x_ref[l:l + 1, :] for l in range(depth)]
    mx = functools.reduce(jnp.maximum, rows)
    es = [jnp.exp(r - mx) for r in rows]
    den = functools.reduce(lambda a, b: a + b, es)
    first = es[0] / den
    run = first
    o_ref[0:1, :] = run - first
    for l in range(1, depth):
        run = run + es[l] / den
        o_ref[l:l + 1, :] = run - first


def _hgrn_lower_bounds(hgrn_lb):
    return pl.pallas_call(
        _lb_kernel,
        out_shape=jax.ShapeDtypeStruct(hgrn_lb.shape, _F32),
        name="hgrn_lower_bounds",
    )(hgrn_lb.astype(_F32))


def _proj_kernel(x_ref, g_ref, sh_ref, sc_ref, w_ref, o_ref, *, n_chunk):
    h = _norm_mod(x_ref[0], g_ref[...], sh_ref[0], sc_ref[0]).astype(_BF16)
    for c in range(0, w_ref.shape[1], n_chunk):
        o_ref[0, :, c:c + n_chunk] = _dot(h, w_ref[:, c:c + n_chunk]).astype(o_ref.dtype)


def _project(x, gain, mod, w, out_dtype, tm):
    bsz, seq, d = x.shape
    n = w.shape[1]
    return pl.pallas_call(
        functools.partial(_proj_kernel, n_chunk=min(n, 1024)),
        grid=(bsz, seq // tm),
        in_specs=[
            pl.BlockSpec((1, tm, d), lambda b, t: (b, t, 0)),
            _resident((1, d)),
            pl.BlockSpec((1, 1, d), lambda b, t: (b, 0, 0)),
            pl.BlockSpec((1, 1, d), lambda b, t: (b, 0, 1)),
            _resident((d, n)),
        ],
        out_specs=pl.BlockSpec((1, tm, n), lambda b, t: (b, t, 0)),
        out_shape=jax.ShapeDtypeStruct((bsz, seq, n), out_dtype),
        compiler_params=_cparams("parallel", "parallel"),
        name="project",
    )(x, gain.reshape(1, d), mod, mod, w)


def _moba_kernel(q_ref, k_ref, v_ref, o_ref, *, blk, topk, scale):
    dh = LANES
    for hd in range(q_ref.shape[2] // dh):
        cols = slice(hd * dh, (hd + 1) * dh)
        _moba_head(q_ref[0, :, cols], k_ref[0, :, cols], v_ref[0, :, cols], o_ref, cols,
                   blk=blk, topk=topk, scale=scale)


def _moba_head(q, k, v, o_ref, cols, *, blk, topk, scale):
    seq, dh = k.shape
    nblk = seq // blk

    row = lax.broadcasted_iota(jnp.int32, (nblk, seq), 0)
    col = lax.broadcasted_iota(jnp.int32, (nblk, seq), 1)
    fully_past = (row + 1) * blk <= col
    kmean = jnp.sum(k.astype(_F32).reshape(nblk, blk, dh), axis=1) * (1.0 / blk)
    kmean_hi = kmean.astype(_BF16)
    kmean_lo = (kmean - kmean_hi.astype(_F32)).astype(_BF16)
    gate_parts = _dot_nt(jnp.concatenate([kmean_hi, kmean_lo], axis=0), q)
    gate_t = gate_parts[:nblk, :] + gate_parts[nblk:, :]

    bias_rows = []
    for j in range(nblk):
        g_j = gate_t[j:j + 1, :]
        beats = jnp.logical_and(
            fully_past, jnp.logical_or(gate_t > g_j, jnp.logical_and(gate_t == g_j, row < j)))
        rank = jnp.sum(beats.astype(_F32), axis=0, keepdims=True)
        masked = jnp.logical_and(fully_past[j:j + 1, :], rank >= topk)
        bias_rows.append(jnp.where(masked, MASK_VALUE, 0.0))
    bias_t = jnp.concatenate(bias_rows + [jnp.zeros((dh - nblk, seq), _F32)], axis=0)

    key_pos = lax.broadcasted_iota(jnp.int32, (seq, dh), 0)
    lane = lax.broadcasted_iota(jnp.int32, (seq, dh), 1)
    onehot = jnp.logical_and(lane * blk <= key_pos, key_pos < (lane + 1) * blk).astype(_BF16)
    k_aug = jnp.concatenate([k, onehot], axis=1)
    v_aug = jnp.concatenate([v, jnp.ones((seq, dh), _BF16)], axis=1)

    qpos = lax.broadcasted_iota(jnp.int32, (blk, blk), 0)
    kpos = lax.broadcasted_iota(jnp.int32, (blk, blk), 1)
    causal = kpos <= qpos
    exp2_scale = scale * 1.4426950408889634
    order = [0] + list(range(nblk - 1, 0, -1))
    logits = {}
    for i in order:
        lo, hi = i * blk, (i + 1) * blk
        if i == 0:
            bias = jnp.zeros((blk, dh), _BF16)
        else:
            bias = jnp.transpose(bias_t[:, lo:hi]).astype(_BF16)
        q_aug = jnp.concatenate([q[lo:hi, :], bias], axis=1)
        logits[i] = _dot_nt(q_aug, k_aug[:hi, :])
    probs = {}
    for i in order:
        lo, hi = i * blk, (i + 1) * blk
        s = logits[i]
        s_own = jnp.where(causal, s[:, lo:hi], MASK_VALUE)
        s = s_own if i == 0 else jnp.concatenate([s[:, :lo], s_own], axis=1)
        m = jnp.max(s, axis=-1, keepdims=True)
        probs[i] = jnp.exp2((s - m) * exp2_scale).astype(_BF16)
    for i in order:
        lo, hi = i * blk, (i + 1) * blk
        o = _dot(probs[i], v_aug[:hi, :])
        o_ref[0, lo:hi, cols] = (o[:, :dh] * (1.0 / o[:, dh:dh + 1])).astype(o_ref.dtype)


def _moba_attention(qkv, heads, heads_per_step):
    bsz, seq, three_hd = qkv.shape
    dh = three_hd // (3 * heads)
    blk = MOBA_BLOCK
    assert seq % blk == 0 and dh == LANES and heads % heads_per_step == 0
    nblk = seq // blk
    topk = min(MOBA_TOPK, nblk - 1)
    groups = heads // heads_per_step
    width = heads_per_step * dh
    kern = functools.partial(_moba_kernel, blk=blk, topk=topk, scale=dh ** -0.5)
    return pl.pallas_call(
        kern,
        grid=(bsz, groups),
        in_specs=[
            pl.BlockSpec((1, seq, width), lambda b, g: (b, 0, g)),
            pl.BlockSpec((1, seq, width), lambda b, g: (b, 0, groups + g)),
            pl.BlockSpec((1, seq, width), lambda b, g: (b, 0, 2 * groups + g)),
        ],
        out_specs=pl.BlockSpec((1, seq, width), lambda b, g: (b, 0, g)),
        out_shape=jax.ShapeDtypeStruct((bsz, seq, heads * dh), _BF16),
        compiler_params=_cparams("parallel", "parallel"),
        name="moba_attention",
    )(qkv, qkv, qkv)


def _split3(x):
    hi = x.astype(_BF16)
    r1 = x - hi.astype(_F32)
    mid = r1.astype(_BF16)
    lo = (r1 - mid.astype(_F32)).astype(_BF16)
    return hi, mid, lo


def _hgrn_kernel(x_ref, gn_ref, sh_ref, sc_ref, w_ref, lb_ref, gain_ref, o_ref, st_ref, *, heads, chunk, sub):
    @pl.when(pl.program_id(1) == 0)
    def _init():
        st_ref[...] = jnp.zeros_like(st_ref)

    fdim = lb_ref.shape[1]
    hv = heads * gain_ref.shape[1]
    tiles = [slice(r, r + sub) for r in range(0, x_ref.shape[1], sub)]

    def project(rows):
        h = _norm_mod(x_ref[0, rows, :], gn_ref[...], sh_ref[0], sc_ref[0]).astype(_BF16)
        return (_dot(h, w_ref[:, :fdim]), _dot(h, w_ref[:, fdim:2 * fdim]),
                _dot(h, w_ref[:, 2 * fdim:2 * fdim + hv]).astype(_BF16), _dot(h, w_ref[:, 2 * fdim + hv:]))

    z = project(tiles[0])
    for i, rows in enumerate(tiles):
        gates = _hgrn_gates(z[0], z[1], lb_ref[...], chunk=chunk)
        v, zg = z[2], z[3]
        if i + 1 < len(tiles):
            z = project(tiles[i + 1])
        _hgrn_heads(gates, v, zg, gain_ref[...], o_ref, rows, st_ref, heads=heads, chunk=chunk)


def _hgrn_gates(zq, zf, lb, *, chunk):
    tm, fdim = zq.shape
    nck = tm // chunk
    q = _silu_tanh(zq)
    fgate = lb + (1.0 - lb) * _sigmoid(zf)
    k = 1.0 - fgate

    shift = chunk.bit_length() - 1
    r_i = lax.broadcasted_iota(jnp.int32, (tm, tm), 0)
    c_i = lax.broadcasted_iota(jnp.int32, (tm, tm), 1)
    block_causal = jnp.logical_and(jnp.right_shift(r_i, shift) == jnp.right_shift(c_i, shift), c_i <= r_i)
    tri = block_causal.astype(_BF16)
    b = functools.reduce(lambda a, c: a + c, [_dot(tri, part) for part in _split3(jnp.log(fgate))])

    by_chunk = lambda t: t.reshape(nck, chunk, fdim)
    b3, q3, k3 = by_chunk(b), by_chunk(q), by_chunk(k)
    b_mid = b3[:, chunk // 2:chunk // 2 + 1, :]
    b_last = b3[:, chunk - 1:chunk, :]
    flat = lambda t: t.astype(_BF16).reshape(tm, fdim)
    qd = flat(q3 * jnp.exp(b3 - b_mid))
    kd = flat(k3 * jnp.exp(b_mid - b3))
    q_in = flat(q3 * jnp.exp(b3))
    k_out = flat(k3 * jnp.exp(b_last - b3))
    decay = jnp.exp(b_last)
    return qd, kd, q_in, k_out, decay, block_causal


def _hgrn_heads(gates, v, zg, gain, o_ref, rows_out, st_ref, *, heads, chunk):
    qd, kd, q_in, k_out, decay, block_causal = gates
    tm, fdim = qd.shape
    nck = tm // chunk
    kdim = fdim // heads
    vdim = gain.shape[1]
    ksl = [slice(hd * kdim, (hd + 1) * kdim) for hd in range(heads)]
    vsl = [slice(hd * vdim, (hd + 1) * vdim) for hd in range(heads)]
    crow = [slice(c * chunk, (c + 1) * chunk) for c in range(nck)]
    causal = block_causal[:chunk, :chunk]
    a_all = [[jnp.where(causal, _dot_nt(qd[rows, ks], kd[rows, ks]), 0.0).astype(_BF16) for rows in crow]
             for ks in ksl]
    v_t = [[jnp.transpose(v[rows, vs].astype(_F32)).astype(_BF16) for rows in crow] for vs in vsl]
    upd = [[_dot(v_t[hd][c], k_out[crow[c], ksl[hd]]) for c in range(nck)] for hd in range(heads)]
    for hd in range(heads):
        st = st_ref[hd]
        outs = []
        for c in range(nck):
            lhs = jnp.concatenate([q_in[crow[c], ksl[hd]], a_all[hd][c]], axis=1)
            rhs_t = jnp.concatenate([st.astype(_BF16), v_t[hd][c]], axis=1)
            outs.append(_dot_nt(lhs, rhs_t))
            st = decay[c, :, ksl[hd]] * st + upd[hd][c]
        st_ref[hd] = st
        o = jnp.concatenate(outs, axis=0)
        zg_h = zg[:, vsl[hd]]
        o_ref[0, rows_out, vsl[hd]] = (_rms_norm(o, gain) * _silu_tanh(zg_h)).astype(o_ref.dtype)


def _hgrn_mixer(x, gain_mix, mod, w_in, lb, g_gain, heads, tm):
    bsz, seq, d = x.shape
    n = w_in.shape[1]
    fdim = lb.shape[-1]
    vdim = g_gain.shape[-1]
    kdim = fdim // heads
    sub = min(tm, HGRN_SUBTILE)
    assert kdim == LANES and vdim == LANES and n == 2 * fdim + 2 * heads * vdim
    assert HGRN_CHUNK & (HGRN_CHUNK - 1) == 0 and sub % HGRN_CHUNK == 0 and tm % sub == 0
    return pl.pallas_call(
        functools.partial(_hgrn_kernel, heads=heads, chunk=HGRN_CHUNK, sub=sub),
        grid=(bsz, seq // tm),
        in_specs=[
            pl.BlockSpec((1, tm, d), lambda b, t: (b, t, 0)),
            _resident((1, d)),
            pl.BlockSpec((1, 1, d), lambda b, t: (b, 0, 0)),
            pl.BlockSpec((1, 1, d), lambda b, t: (b, 0, 1)),
            _resident((d, n)),
            _resident((1, fdim)),
            _resident((1, vdim)),
        ],
        out_specs=pl.BlockSpec((1, tm, heads * vdim), lambda b, t: (b, t, 0)),
        out_shape=jax.ShapeDtypeStruct((bsz, seq, heads * vdim), _BF16),
        scratch_shapes=[pltpu.VMEM((heads, vdim, kdim), _F32)],
        compiler_params=_cparams("parallel", "arbitrary"),
        name="hgrn2_mixer",
    )(x, gain_mix.reshape(1, d), mod, mod, w_in, lb.reshape(1, fdim), g_gain.reshape(1, vdim))


def _gelu_tanh(x):
    half = 0.5 * x
    return half + half * jnp.tanh(x * (0.7978845608028654 + 0.035677408136300125 * (x * x)))


def _rglru_kernel(x_ref, gn_ref, sh_ref, sc_ref, w_ref, cw_ref, cb_ref, wa_ref, ba_ref, wi_ref, bi_ref,
                  lam_ref, o_ref, hc_ref, xh_ref, *, phases, sub):
    t = pl.program_id(1)

    @pl.when(t == 0)
    def _init():
        hc_ref[...] = jnp.zeros_like(hc_ref)
        xh_ref[...] = jnp.zeros_like(xh_ref)

    width = o_ref.shape[2]
    ng = sub // phases
    shift = ng.bit_length() - 1
    i_r = lax.broadcasted_iota(jnp.int32, (sub, sub), 0)
    i_c = lax.broadcasted_iota(jnp.int32, (sub, sub), 1)
    time_of = lambda i: jnp.bitwise_and(i, ng - 1) * phases + jnp.right_shift(i, shift)
    perm = (i_c == time_of(i_r)).astype(_BF16)
    unperm = (i_r == time_of(i_c)).astype(_BF16)
    slab = lambda val, p: val[p * ng:(p + 1) * ng, :]
    grow = lax.broadcasted_iota(jnp.int32, (ng, 1), 0)
    row = lax.broadcasted_iota(jnp.int32, (sub, 1), 0)
    taps = cw_ref.shape[0]
    nb = wa_ref.shape[0]
    bw = width // nb
    neg_lam = -lam_ref[...]
    softplus = jnp.maximum(neg_lam, 0.0) + jnp.log1p(jnp.exp(-jnp.abs(neg_lam)))
    log_a_per_r = -RG_C * softplus

    def project(rows):
        h = _norm_mod(x_ref[0, rows, :], gn_ref[...], sh_ref[0], sc_ref[0]).astype(_BF16)
        h = _dot(perm, h).astype(_BF16)
        return _dot(h, w_ref[:, :width]), _dot(h, w_ref[:, width:])

    tiles = [slice(r, r + sub) for r in range(0, x_ref.shape[1], sub)]
    z = project(tiles[0])
    for i, rows in enumerate(tiles):
        y_pre, x_br = z
        prev_group = {p: jnp.where(grow >= 1, pltpu.roll(slab(x_br, p), 1, 0), xh_ref[p:p + 1, :])
                      for p in range(phases - (taps - 1), phases)}
        for p in range(phases - (taps - 1), phases):
            xh_ref[p:p + 1, :] = slab(x_br, p)[ng - 1:ng, :]
        conv_slabs = []
        for p in range(phases):
            acc = cb_ref[...]
            for j in range(taps):
                src = p - (taps - 1) + j
                acc = acc + (slab(x_br, src) if src >= 0 else prev_group[src + phases]) * cw_ref[j:j + 1, :]
            conv_slabs.append(acc)
        x_conv = jnp.concatenate(conv_slabs, axis=0)
        if i + 1 < len(tiles):
            z = project(tiles[i + 1])

        xcb = x_conv.astype(_BF16)
        r = jnp.concatenate([_dot(xcb[:, n * bw:(n + 1) * bw], wa_ref[n]) for n in range(nb)], axis=1)
        gi = jnp.concatenate([_dot(xcb[:, n * bw:(n + 1) * bw], wi_ref[n]) for n in range(nb)], axis=1)
        r = _sigmoid_tanh(r + ba_ref[...])
        gi = _sigmoid_tanh(gi + bi_ref[...])
        log_a = r * log_a_per_r
        a = jnp.exp(log_a)
        th = jnp.tanh(log_a)
        mult = jnp.sqrt(-2.0 * th) * lax.rsqrt(1.0 - th)
        if i == 0:
            mult = jnp.where(jnp.logical_and(row == 0, t == 0), 1.0, mult)
        u = (gi * x_conv) * mult

        h_loc, a_cum = [slab(u, 0)], [slab(a, 0)]
        for p in range(1, phases):
            h_loc.append(slab(a, p) * h_loc[-1] + slab(u, p))
            a_cum.append(slab(a, p) * a_cum[-1])
        ga, gu = a_cum[-1], h_loc[-1]
        d = 1
        while d < ng:
            keep = grow >= d
            a_sh = jnp.where(keep, pltpu.roll(ga, d, 0), 1.0)
            u_sh = jnp.where(keep, pltpu.roll(gu, d, 0), 0.0)
            ga, gu = ga * a_sh, gu + ga * u_sh
            d *= 2
        carry = hc_ref[...]
        h_end = gu + ga * carry
        hc_ref[...] = h_end[ng - 1:ng, :]
        h_in = jnp.where(grow >= 1, pltpu.roll(h_end, 1, 0), carry)
        hs = jnp.concatenate([h_loc[p] + a_cum[p] * h_in for p in range(phases)], axis=0)
        out = (hs * _gelu_tanh(y_pre)).astype(_BF16)
        o_ref[0, rows, :] = _dot(unperm, out).astype(o_ref.dtype)


def _rglru_mixer(x, gain_mix, mod, w_in, conv_w, conv_b, w_a, b_a, w_i, b_i, lam, tm):
    bsz, seq, d = x.shape
    two_w = w_in.shape[1]
    width = two_w // 2
    nb, bw, _ = w_a.shape
    phases = RG_SCAN_PHASES
    sub = min(tm, RG_SUBTILE)
    ng = sub // phases
    assert tm % sub == 0 and sub % phases == 0 and ng % SUBLANES == 0 and ng & (ng - 1) == 0
    assert phases >= RG_CONV_WIDTH
    row = lambda v: v.reshape(1, width)
    return pl.pallas_call(
        functools.partial(_rglru_kernel, phases=phases, sub=sub),
        grid=(bsz, seq // tm),
        in_specs=[
            pl.BlockSpec((1, tm, d), lambda b, t: (b, t, 0)),
            _resident((1, d)),
            pl.BlockSpec((1, 1, d), lambda b, t: (b, 0, 0)),
            pl.BlockSpec((1, 1, d), lambda b, t: (b, 0, 1)),
            _resident((d, two_w)),
            _resident((RG_CONV_WIDTH, width)),
            _resident((1, width)),
            _resident((nb, bw, bw)),
            _resident((1, width)),
            _resident((nb, bw, bw)),
            _resident((1, width)),
            _resident((1, width)),
        ],
        out_specs=pl.BlockSpec((1, tm, width), lambda b, t: (b, t, 0)),
        out_shape=jax.ShapeDtypeStruct((bsz, seq, width), _BF16),
        scratch_shapes=[
            pltpu.VMEM((1, width), _F32),
            pltpu.VMEM((phases, width), _F32),
        ],
        compiler_params=_cparams("parallel", "arbitrary"),
        name="rglru_mixer",
    )(x, gain_mix.reshape(1, d), mod, mod, w_in, conv_w, row(conv_b), w_a.astype(_BF16), row(b_a),
      w_i.astype(_BF16), row(b_i), row(lam))


def _tail_kernel(x_ref, m_ref, wo_ref, g1_ref, gn_ref, sh_ref, sc_ref, g2_ref, wup_ref, wdn_ref,
                 *rest, ff_chunk, final):
    o_ref = rest[-1]
    x1 = x_ref[0] + g1_ref[0] * _dot(m_ref[0], wo_ref[...])
    h = _norm_mod(x1, gn_ref[...], sh_ref[0], sc_ref[0]).astype(_BF16)
    acc = jnp.zeros_like(x1)
    for c in range(0, wup_ref.shape[1], ff_chunk):
        u = jnp.maximum(_dot(h, wup_ref[:, c:c + ff_chunk]), 0.0)
        acc = acc + _dot((u * u).astype(_BF16), wdn_ref[c:c + ff_chunk, :])
    x2 = x1 + g2_ref[0] * acc
    if final:
        x2 = _rms_norm(x2, rest[0][...])
    o_ref[0] = x2


def _layer_tail(x, m, w_o, mod, gain_mlp, w_up, w_down, final_gain, tm):
    bsz, seq, d = x.shape
    dm = m.shape[-1]
    dff = w_up.shape[1]
    final = final_gain is not None
    mod_spec = lambda idx: pl.BlockSpec((1, 1, d), lambda b, t: (b, 0, idx))
    in_specs = [
        pl.BlockSpec((1, tm, d), lambda b, t: (b, t, 0)),
        pl.BlockSpec((1, tm, dm), lambda b, t: (b, t, 0)),
        _resident((dm, d)),
        mod_spec(2),
        _resident((1, d)),
        mod_spec(3),
        mod_spec(4),
        mod_spec(5),
        _resident((d, dff)),
        _resident((dff, d)),
    ]
    args = [x, m, w_o, mod, gain_mlp.reshape(1, d), mod, mod, mod, w_up, w_down]
    if final:
        in_specs.append(_resident((1, d)))
        args.append(final_gain.reshape(1, d))
    return pl.pallas_call(
        functools.partial(_tail_kernel, ff_chunk=min(dff, 1024), final=final),
        grid=(bsz, seq // tm),
        in_specs=in_specs,
        out_specs=pl.BlockSpec((1, tm, d), lambda b, t: (b, t, 0)),
        out_shape=jax.ShapeDtypeStruct((bsz, seq, d), _F32),
        compiler_params=_cparams("parallel", "parallel"),
        name="layer_tail",
    )(*args)


def kernel(x, c, ada_w, ada_b, norm_mix, norm_mlp, mlp_up, mlp_down, moba_wqkv, moba_wo, hgrn_w_in, hgrn_lb, hgrn_norm, hgrn_wo, rg_w_in, rg_conv_w, rg_conv_b, rg_w_a, rg_b_a, rg_w_i, rg_b_i, rg_lambda, rg_wo, final_norm):
    depth = ada_w.shape[0]
    bsz, seq, d = x.shape
    tiles = _tiles(seq)
    bf = lambda w: w.astype(_BF16)

    mod_all = _ada_modulation(c, ada_w, ada_b)
    lb_all = _hgrn_lower_bounds(hgrn_lb)
    i_a = i_b = i_c = 0
    for layer in range(depth):
        mod = mod_all[layer].reshape(bsz, 1, 6 * d)
        kind = layer % N_MIXERS
        if kind == 0:
            qkv = _project(x, norm_mix[layer], mod, bf(moba_wqkv[i_a]), _BF16, tiles["proj"])
            m = _moba_attention(qkv, MOBA_HEADS, tiles["moba_heads"])
            w_o = moba_wo[i_a]
            i_a += 1
        elif kind == 1:
            m = _hgrn_mixer(x, norm_mix[layer], mod, bf(hgrn_w_in[i_b]), lb_all[layer], hgrn_norm[i_b],
                            HGRN_HEADS, tiles["hgrn"])
            w_o = hgrn_wo[i_b]
            i_b += 1
        else:
            m = _rglru_mixer(x, norm_mix[layer], mod, bf(rg_w_in[i_c]), rg_conv_w[i_c], rg_conv_b[i_c],
                             rg_w_a[i_c], rg_b_a[i_c], rg_w_i[i_c], rg_b_i[i_c], rg_lambda[i_c],
                             tiles["rglru"])
            w_o = rg_wo[i_c]
            i_c += 1
        x = _layer_tail(x, m, bf(w_o), mod, norm_mlp[layer], bf(mlp_up[layer]), bf(mlp_down[layer]),
                        final_norm if layer == depth - 1 else None, tiles["tail"])
    return x
```

```python
import functools

import jax
import jax.numpy as jnp
from jax import lax
from jax.experimental import pallas as pl
from jax.experimental.pallas import tpu as pltpu

_F32 = jnp.float32
_BF16 = jnp.bfloat16

NORM_EPS = 1e-6
N_MIXERS = 3
MOBA_HEADS = 8
MOBA_BLOCK = 256
MOBA_TOPK = 3
HGRN_HEADS = 8
HGRN_CHUNK = 64
HGRN_SUBTILE = 256
RG_BLOCKS = 4
RG_CONV_WIDTH = 4
RG_C = 8.0
RG_SUBTILE = 256
RG_SCAN_PHASES = 8
MASK_VALUE = -1e30

V7X_VMEM_LIMIT_BYTES = 56 * 1024 * 1024
LANES = 128
SUBLANES = 8


def _tiles(seq):
    return dict(proj=min(seq, 1024), tail=min(seq, 1024), hgrn=min(seq, 1024), rglru=min(seq, 1024), moba_heads=4)


def _cparams(*sem):
    return pltpu.CompilerParams(dimension_semantics=sem, vmem_limit_bytes=V7X_VMEM_LIMIT_BYTES)


def _resident(shape):
    zeros = (0,) * len(shape)
    return pl.BlockSpec(shape, lambda *_: zeros, pipeline_mode=pl.Buffered(1))


def _sigmoid(x):
    return 1.0 / (1.0 + jnp.exp(-x))


def _sigmoid_tanh(x):
    return 0.5 * jnp.tanh(0.5 * x) + 0.5


def _silu_tanh(x):
    half = 0.5 * x
    return half + half * jnp.tanh(half)


def _rms_norm(x, gain):
    return x * lax.rsqrt(jnp.mean(x * x, axis=-1, keepdims=True) + NORM_EPS) * gain


def _norm_mod(x, gain, shift, scale):
    return x * lax.rsqrt(jnp.mean(x * x, axis=-1, keepdims=True) + NORM_EPS) * (gain * (1.0 + scale)) + shift


def _dot(a, b):
    return jnp.dot(a, b, preferred_element_type=_F32)


def _dot_nt(a, b):
    return lax.dot_general(a, b, (((1,), (1,)), ((), ())), preferred_element_type=_F32)


def _ada_kernel(c_ref, w_ref, b_ref, o_ref):
    c = c_ref[...]
    cond = c * _sigmoid(c)
    bsz = cond.shape[0]
    w = w_ref[0]
    c_hi = cond.astype(_BF16)
    c_lo = (cond - c_hi.astype(_F32)).astype(_BF16)
    w_hi = w.astype(_BF16)
    w_lo = (w - w_hi.astype(_F32)).astype(_BF16)
    with_hi = _dot(jnp.concatenate([c_hi, c_lo], axis=0), w_hi)
    o_ref[0] = with_hi[:bsz] + with_hi[bsz:] + _dot(c_hi, w_lo) + b_ref[0]


def _ada_modulation(c, ada_w, ada_b):
    depth, d, n = ada_w.shape
    bsz = c.shape[0]
    tn = min(n, 1536)
    return pl.pallas_call(
        _ada_kernel,
        grid=(depth, n // tn),
        in_specs=[
            pl.BlockSpec((bsz, d), lambda l, j: (0, 0)),
            pl.BlockSpec((1, d, tn), lambda l, j: (l, 0, j)),
            pl.BlockSpec((1, 1, tn), lambda l, j: (l, 0, j)),
        ],
        out_specs=pl.BlockSpec((1, bsz, tn), lambda l, j: (l, 0, j)),
        out_shape=jax.ShapeDtypeStruct((depth, bsz, n), _F32),
        compiler_params=_cparams("arbitrary", "arbitrary"),
        name="ada_modulation",
    )(c, ada_w, ada_b.reshape(depth, 1, n))


def _lb_kernel(x_ref, o_ref):
    depth = x_ref.shape[0]
    rows = [x_ref[l:l + 1, :] for l in range(depth)]
    mx = functools.reduce(jnp.maximum, rows)
    es = [jnp.exp(r - mx) for r in rows]
    den = functools.reduce(lambda a, b: a + b, es)
    first = es[0] / den
    run = first
    o_ref[0:1, :] = run - first
    for l in range(1, depth):
        run = run + es[l] / den
        o_ref[l:l + 1, :] = run - first


def _hgrn_lower_bounds(hgrn_lb):
    return pl.pallas_call(
        _lb_kernel,
        out_shape=jax.ShapeDtypeStruct(hgrn_lb.shape, _F32),
        name="hgrn_lower_bounds",
    )(hgrn_lb.astype(_F32))


def _proj_kernel(x_ref, g_ref, sh_ref, sc_ref, w_ref, o_ref, *, n_chunk):
    h = _norm_mod(x_ref[0], g_ref[...], sh_ref[0], sc_ref[0]).astype(_BF16)
    for c in range(0, w_ref.shape[1], n_chunk):
        o_ref[0, :, c:c + n_chunk] = _dot(h, w_ref[:, c:c + n_chunk]).astype(o_ref.dtype)


def _project(x, gain, mod, w, out_dtype, tm):
    bsz, seq, d = x.shape
    n = w.shape[1]
    return pl.pallas_call(
        functools.partial(_proj_kernel, n_chunk=min(n, 1024)),
        grid=(bsz, seq // tm),
        in_specs=[
            pl.BlockSpec((1, tm, d), lambda b, t: (b, t, 0)),
            _resident((1, d)),
            pl.BlockSpec((1, 1, d), lambda b, t: (b, 0, 0)),
            pl.BlockSpec((1, 1, d), lambda b, t: (b, 0, 1)),
            _resident((d, n)),
        ],
        out_specs=pl.BlockSpec((1, tm, n), lambda b, t: (b, t, 0)),
        out_shape=jax.ShapeDtypeStruct((bsz, seq, n), out_dtype),
        compiler_params=_cparams("parallel", "parallel"),
        name="project",
    )(x, gain.reshape(1, d), mod, mod, w)


def _moba_kernel(q_ref, k_ref, v_ref, o_ref, *, blk, topk, scale):
    dh = LANES
    for hd in range(q_ref.shape[2] // dh):
        cols = slice(hd * dh, (hd + 1) * dh)
        _moba_head(q_ref[0, :, cols], k_ref[0, :, cols], v_ref[0, :, cols], o_ref, cols,
                   blk=blk, topk=topk, scale=scale)


def _moba_head(q, k, v, o_ref, cols, *, blk, topk, scale):
    seq, dh = k.shape
    nblk = seq // blk

    row = lax.broadcasted_iota(jnp.int32, (nblk, seq), 0)
    col = lax.broadcasted_iota(jnp.int32, (nblk, seq), 1)
    fully_past = (row + 1) * blk <= col
    kmean = jnp.sum(k.astype(_F32).reshape(nblk, blk, dh), axis=1) * (1.0 / blk)
    kmean_hi = kmean.astype(_BF16)
    kmean_lo = (kmean - kmean_hi.astype(_F32)).astype(_BF16)
    gate_parts = _dot_nt(jnp.concatenate([kmean_hi, kmean_lo], axis=0), q)
    gate_t = gate_parts[:nblk, :] + gate_parts[nblk:, :]

    bias_rows = []
    for j in range(nblk):
        g_j = gate_t[j:j + 1, :]
        beats = jnp.logical_and(
            fully_past, jnp.logical_or(gate_t > g_j, jnp.logical_and(gate_t == g_j, row < j)))
        rank = jnp.sum(beats.astype(_F32), axis=0, keepdims=True)
        masked = jnp.logical_and(fully_past[j:j + 1, :], rank >= topk)
        bias_rows.append(jnp.where(masked, MASK_VALUE, 0.0))
    bias_t = jnp.concatenate(bias_rows + [jnp.zeros((dh - nblk, seq), _F32)], axis=0)

    key_pos = lax.broadcasted_iota(jnp.int32, (seq, dh), 0)
    lane = lax.broadcasted_iota(jnp.int32, (seq, dh), 1)
    onehot = jnp.logical_and(lane * blk <= key_pos, key_pos < (lane + 1) * blk).astype(_BF16)
    k_exp2 = (k.astype(_F32) * (scale * 1.4426950408889634)).astype(_BF16)
    k_aug = jnp.concatenate([k_exp2, onehot], axis=1)
    v_aug = jnp.concatenate([v, jnp.ones((seq, dh), _BF16)], axis=1)

    qpos = lax.broadcasted_iota(jnp.int32, (blk, blk), 0)
    kpos = lax.broadcasted_iota(jnp.int32, (blk, blk), 1)
    causal = kpos <= qpos
    order = [0] + list(range(nblk - 1, 0, -1))
    logits = {}
    for i in order:
        lo, hi = i * blk, (i + 1) * blk
        if i == 0:
            bias = jnp.zeros((blk, dh), _BF16)
        else:
            bias = jnp.transpose(bias_t[:, lo:hi]).astype(_BF16)
        q_aug = jnp.concatenate([q[lo:hi, :], bias], axis=1)
        logits[i] = _dot_nt(q_aug, k_aug[:hi, :]).astype(_BF16)
    probs = {}
    for i in order:
        lo, hi = i * blk, (i + 1) * blk
        s = logits[i]
        s_own = jnp.where(causal, s[:, lo:hi], jnp.asarray(MASK_VALUE, _BF16))
        s = s_own if i == 0 else jnp.concatenate([s[:, :lo], s_own], axis=1)
        m = jnp.max(s, axis=-1, keepdims=True)
        probs[i] = jnp.exp2(s - m)
    for i in order:
        lo, hi = i * blk, (i + 1) * blk
        o = _dot(probs[i], v_aug[:hi, :])
        o_ref[0, lo:hi, cols] = (o[:, :dh] * (1.0 / o[:, dh:dh + 1])).astype(o_ref.dtype)


def _moba_attention(qkv, heads, heads_per_step):
    bsz, seq, three_hd = qkv.shape
    dh = three_hd // (3 * heads)
    blk = MOBA_BLOCK
    assert seq % blk == 0 and dh == LANES and heads % heads_per_step == 0
    nblk = seq // blk
    topk = min(MOBA_TOPK, nblk - 1)
    groups = heads // heads_per_step
    width = heads_per_step * dh
    kern = functools.partial(_moba_kernel, blk=blk, topk=topk, scale=dh ** -0.5)
    return pl.pallas_call(
        kern,
        grid=(bsz, groups),
        in_specs=[
            pl.BlockSpec((1, seq, width), lambda b, g: (b, 0, g)),
            pl.BlockSpec((1, seq, width), lambda b, g: (b, 0, groups + g)),
            pl.BlockSpec((1, seq, width), lambda b, g: (b, 0, 2 * groups + g)),
        ],
        out_specs=pl.BlockSpec((1, seq, width), lambda b, g: (b, 0, g)),
        out_shape=jax.ShapeDtypeStruct((bsz, seq, heads * dh), _BF16),
        compiler_params=_cparams("parallel", "parallel"),
        name="moba_attention",
    )(qkv, qkv, qkv)


def _split3(x):
    hi = x.astype(_BF16)
    r1 = x - hi.astype(_F32)
    mid = r1.astype(_BF16)
    lo = (r1 - mid.astype(_F32)).astype(_BF16)
    return hi, mid, lo


def _hgrn_kernel(x_ref, gn_ref, sh_ref, sc_ref, w_ref, lb_ref, gain_ref, o_ref, st_ref, *, heads, chunk, sub):
    @pl.when(pl.program_id(1) == 0)
    def _init():
        st_ref[...] = jnp.zeros_like(st_ref)

    fdim = lb_ref.shape[1]
    hv = heads * gain_ref.shape[1]
    tiles = [slice(r, r + sub) for r in range(0, x_ref.shape[1], sub)]

    def project(rows):
        h = _norm_mod(x_ref[0, rows, :], gn_ref[...], sh_ref[0], sc_ref[0]).astype(_BF16)
        return (_dot(h, w_ref[:, :fdim]), _dot(h, w_ref[:, fdim:2 * fdim]),
                _dot(h, w_ref[:, 2 * fdim:2 * fdim + hv]).astype(_BF16), _dot(h, w_ref[:, 2 * fdim + hv:]))

    z = project(tiles[0])
    for i, rows in enumerate(tiles):
        gates = _hgrn_gates(z[0], z[1], lb_ref[...], chunk=chunk)
        v, zg = z[2], z[3]
        if i + 1 < len(tiles):
            z = project(tiles[i + 1])
        _hgrn_heads(gates, v, zg, gain_ref[...], o_ref, rows, st_ref, heads=heads, chunk=chunk)


def _hgrn_gates(zq, zf, lb, *, chunk):
    tm, fdim = zq.shape
    nck = tm // chunk
    q = _silu_tanh(zq)
    fgate = lb + (1.0 - lb) * _sigmoid(zf)
    k = 1.0 - fgate

    shift = chunk.bit_length() - 1
    r_i = lax.broadcasted_iota(jnp.int32, (tm, tm), 0)
    c_i = lax.broadcasted_iota(jnp.int32, (tm, tm), 1)
    block_causal = jnp.logical_and(jnp.right_shift(r_i, shift) == jnp.right_shift(c_i, shift), c_i <= r_i)
    tri = block_causal.astype(_BF16)
    b = functools.reduce(lambda a, c: a + c, [_dot(tri, part) for part in _split3(jnp.log(fgate))])

    by_chunk = lambda t: t.reshape(nck, chunk, fdim)
    b3, q3, k3 = by_chunk(b), by_chunk(q), by_chunk(k)
    b_mid = b3[:, chunk // 2:chunk // 2 + 1, :]
    b_last = b3[:, chunk - 1:chunk, :]
    flat = lambda t: t.astype(_BF16).reshape(tm, fdim)
    qd = flat(q3 * jnp.exp(b3 - b_mid))
    kd = flat(k3 * jnp.exp(b_mid - b3))
    q_in = flat(q3 * jnp.exp(b3))
    k_out = flat(k3 * jnp.exp(b_last - b3))
    decay = jnp.exp(b_last)
    return qd, kd, q_in, k_out, decay, block_causal


def _hgrn_heads(gates, v, zg, gain, o_ref, rows_out, st_ref, *, heads, chunk):
    qd, kd, q_in, k_out, decay, block_causal = gates
    tm, fdim = qd.shape
    nck = tm // chunk
    kdim = fdim // heads
    vdim = gain.shape[1]
    ksl = [slice(hd * kdim, (hd + 1) * kdim) for hd in range(heads)]
    vsl = [slice(hd * vdim, (hd + 1) * vdim) for hd in range(heads)]
    crow = [slice(c * chunk, (c + 1) * chunk) for c in range(nck)]
    causal = block_causal[:chunk, :chunk]
    a_all = [[jnp.where(causal, _dot_nt(qd[rows, ks], kd[rows, ks]), 0.0).astype(_BF16) for rows in crow]
             for ks in ksl]
    v_t = [[jnp.transpose(v[rows, vs].astype(_F32)).astype(_BF16) for rows in crow] for vs in vsl]
    upd = [[_dot(v_t[hd][c], k_out[crow[c], ksl[hd]]) for c in range(nck)] for hd in range(heads)]
    for hd in range(heads):
        st = st_ref[hd]
        outs = []
        for c in range(nck):
            lhs = jnp.concatenate([q_in[crow[c], ksl[hd]], a_all[hd][c]], axis=1)
            rhs_t = jnp.concatenate([st.astype(_BF16), v_t[hd][c]], axis=1)
            outs.append(_dot_nt(lhs, rhs_t))
            st = decay[c, :, ksl[hd]] * st + upd[hd][c]
        st_ref[hd] = st
        o = jnp.concatenate(outs, axis=0)
        zg_h = zg[:, vsl[hd]]
        o_ref[0, rows_out, vsl[hd]] = (_rms_norm(o, gain) * _silu_tanh(zg_h)).astype(o_ref.dtype)


def _hgrn_mixer(x, gain_mix, mod, w_in, lb, g_gain, heads, tm):
    bsz, seq, d = x.shape
    n = w_in.shape[1]
    fdim = lb.shape[-1]
    vdim = g_gain.shape[-1]
    kdim = fdim // heads
    sub = min(tm, HGRN_SUBTILE)
    assert kdim == LANES and vdim == LANES and n == 2 * fdim + 2 * heads * vdim
    assert HGRN_CHUNK & (HGRN_CHUNK - 1) == 0 and sub % HGRN_CHUNK == 0 and tm % sub == 0
    return pl.pallas_call(
        functools.partial(_hgrn_kernel, heads=heads, chunk=HGRN_CHUNK, sub=sub),
        grid=(bsz, seq // tm),
        in_specs=[
            pl.BlockSpec((1, tm, d), lambda b, t: (b, t, 0)),
            _resident((1, d)),
            pl.BlockSpec((1, 1, d), lambda b, t: (b, 0, 0)),
            pl.BlockSpec((1, 1, d), lambda b, t: (b, 0, 1)),
            _resident((d, n)),
            _resident((1, fdim)),
            _resident((1, vdim)),
        ],
        out_specs=pl.BlockSpec((1, tm, heads * vdim), lambda b, t: (b, t, 0)),
        out_shape=jax.ShapeDtypeStruct((bsz, seq, heads * vdim), _BF16),
        scratch_shapes=[pltpu.VMEM((heads, vdim, kdim), _F32)],
        compiler_params=_cparams("parallel", "arbitrary"),
        name="hgrn2_mixer",
    )(x, gain_mix.reshape(1, d), mod, mod, w_in, lb.reshape(1, fdim), g_gain.reshape(1, vdim))


def _gelu_tanh(x):
    half = 0.5 * x
    return half + half * jnp.tanh(x * (0.7978845608028654 + 0.035677408136300125 * (x * x)))


def _rglru_kernel(x_ref, gn_ref, sh_ref, sc_ref, w_ref, cw_ref, cb_ref, wa_ref, ba_ref, wi_ref, bi_ref,
                  lam_ref, o_ref, hc_ref, xh_ref, *, phases, sub):
    t = pl.program_id(1)

    @pl.when(t == 0)
    def _init():
        hc_ref[...] = jnp.zeros_like(hc_ref)
        xh_ref[...] = jnp.zeros_like(xh_ref)

    width = o_ref.shape[2]
    ng = sub // phases
    shift = ng.bit_length() - 1
    i_r = lax.broadcasted_iota(jnp.int32, (sub, sub), 0)
    i_c = lax.broadcasted_iota(jnp.int32, (sub, sub), 1)
    time_of = lambda i: jnp.bitwise_and(i, ng - 1) * phases + jnp.right_shift(i, shift)
    perm = (i_c == time_of(i_r)).astype(_BF16)
    unperm = (i_r == time_of(i_c)).astype(_BF16)
    slab = lambda val, p: val[p * ng:(p + 1) * ng, :]
    grow = lax.broadcasted_iota(jnp.int32, (ng, 1), 0)
    row = lax.broadcasted_iota(jnp.int32, (sub, 1), 0)
    taps = cw_ref.shape[0]
    nb = wa_ref.shape[0]
    bw = width // nb
    neg_lam = -lam_ref[...]
    softplus = jnp.maximum(neg_lam, 0.0) + jnp.log1p(jnp.exp(-jnp.abs(neg_lam)))
    log_a_per_r = -RG_C * softplus

    def project(rows):
        h = _norm_mod(x_ref[0, rows, :], gn_ref[...], sh_ref[0], sc_ref[0]).astype(_BF16)
        h = _dot(perm, h).astype(_BF16)
        return _dot(h, w_ref[:, :width]), _dot(h, w_ref[:, width:])

    tiles = [slice(r, r + sub) for r in range(0, x_ref.shape[1], sub)]
    z = project(tiles[0])
    for i, rows in enumerate(tiles):
        y_pre, x_br = z
        prev_group = {p: jnp.where(grow >= 1, pltpu.roll(slab(x_br, p), 1, 0), xh_ref[p:p + 1, :])
                      for p in range(phases - (taps - 1), phases)}
        for p in range(phases - (taps - 1), phases):
            xh_ref[p:p + 1, :] = slab(x_br, p)[ng - 1:ng, :]
        conv_slabs = []
        for p in range(phases):
            acc = cb_ref[...]
            for j in range(taps):
                src = p - (taps - 1) + j
                acc = acc + (slab(x_br, src) if src >= 0 else prev_group[src + phases]) * cw_ref[j:j + 1, :]
            conv_slabs.append(acc)
        x_conv = jnp.concatenate(conv_slabs, axis=0)
        if i + 1 < len(tiles):
            z = project(tiles[i + 1])

        xcb = x_conv.astype(_BF16)
        r = jnp.concatenate([_dot(xcb[:, n * bw:(n + 1) * bw], wa_ref[n]) for n in range(nb)], axis=1)
        gi = jnp.concatenate([_dot(xcb[:, n * bw:(n + 1) * bw], wi_ref[n]) for n in range(nb)], axis=1)
        r = _sigmoid_tanh(r + ba_ref[...])
        gi = _sigmoid_tanh(gi + bi_ref[...])
        log_a = r * log_a_per_r
        a = jnp.exp(log_a)
        th = jnp.tanh(log_a)
        mult = jnp.sqrt(-2.0 * th) * lax.rsqrt(1.0 - th)
        if i == 0:
            mult = jnp.where(jnp.logical_and(row == 0, t == 0), 1.0, mult)
        u = (gi * x_conv) * mult

        h_loc, a_cum = [slab(u, 0)], [slab(a, 0)]
        for p in range(1, phases):
            h_loc.append(slab(a, p) * h_loc[-1] + slab(u, p))
            a_cum.append(slab(a, p) * a_cum[-1])
        ga, gu = a_cum[-1], h_loc[-1]
        d = 1
        while d < ng:
            keep = grow >= d
            a_sh = jnp.where(keep, pltpu.roll(ga, d, 0), 1.0)
            u_sh = jnp.where(keep, pltpu.roll(gu, d, 0), 0.0)
            ga, gu = ga * a_sh, gu + ga * u_sh
            d *= 2
        carry = hc_ref[...]
        h_end = gu + ga * carry
        hc_ref[...] = h_end[ng - 1:ng, :]
        h_in = jnp.where(grow >= 1, pltpu.roll(h_end, 1, 0), carry)
        hs = jnp.concatenate([h_loc[p] + a_cum[p] * h_in for p in range(phases)], axis=0)
        out = (hs * _gelu_tanh(y_pre)).astype(_BF16)
        o_ref[0, rows, :] = _dot(unperm, out).astype(o_ref.dtype)


def _rglru_mixer(x, gain_mix, mod, w_in, conv_w, conv_b, w_a, b_a, w_i, b_i, lam, tm):
    bsz, seq, d = x.shape
    two_w = w_in.shape[1]
    width = two_w // 2
    nb, bw, _ = w_a.shape
    phases = RG_SCAN_PHASES
    sub = min(tm, RG_SUBTILE)
    ng = sub // phases
    assert tm % sub == 0 and sub % phases == 0 and ng % SUBLANES == 0 and ng & (ng - 1) == 0
    assert phases >= RG_CONV_WIDTH
    row = lambda v: v.reshape(1, width)
    return pl.pallas_call(
        functools.partial(_rglru_kernel, phases=phases, sub=sub),
        grid=(bsz, seq // tm),
        in_specs=[
            pl.BlockSpec((1, tm, d), lambda b, t: (b, t, 0)),
            _resident((1, d)),
            pl.BlockSpec((1, 1, d), lambda b, t: (b, 0, 0)),
            pl.BlockSpec((1, 1, d), lambda b, t: (b, 0, 1)),
            _resident((d, two_w)),
            _resident((RG_CONV_WIDTH, width)),
            _resident((1, width)),
            _resident((nb, bw, bw)),
            _resident((1, width)),
            _resident((nb, bw, bw)),
            _resident((1, width)),
            _resident((1, width)),
        ],
        out_specs=pl.BlockSpec((1, tm, width), lambda b, t: (b, t, 0)),
        out_shape=jax.ShapeDtypeStruct((bsz, seq, width), _BF16),
        scratch_shapes=[
            pltpu.VMEM((1, width), _F32),
            pltpu.VMEM((phases, width), _F32),
        ],
        compiler_params=_cparams("parallel", "arbitrary"),
        name="rglru_mixer",
    )(x, gain_mix.reshape(1, d), mod, mod, w_in, conv_w, row(conv_b), w_a.astype(_BF16), row(b_a),
      w_i.astype(_BF16), row(b_i), row(lam))


def _tail_kernel(x_ref, m_ref, wo_ref, g1_ref, gn_ref, sh_ref, sc_ref, g2_ref, wup_ref, wdn_ref,
                 *rest, ff_chunk, final):
    o_ref = rest[-1]
    x1 = x_ref[0] + g1_ref[0] * _dot(m_ref[0], wo_ref[...])
    h = _norm_mod(x1, gn_ref[...], sh_ref[0], sc_ref[0]).astype(_BF16)
    acc = jnp.zeros_like(x1)
    for c in range(0, wup_ref.shape[1], ff_chunk):
        u = jnp.maximum(_dot(h, wup_ref[:, c:c + ff_chunk]), 0.0)
        acc = acc + _dot((u * u).astype(_BF16), wdn_ref[c:c + ff_chunk, :])
    x2 = x1 + g2_ref[0] * acc
    if final:
        x2 = _rms_norm(x2, rest[0][...])
    o_ref[0] = x2


def _layer_tail(x, m, w_o, mod, gain_mlp, w_up, w_down, final_gain, tm):
    bsz, seq, d = x.shape
    dm = m.shape[-1]
    dff = w_up.shape[1]
    final = final_gain is not None
    mod_spec = lambda idx: pl.BlockSpec((1, 1, d), lambda b, t: (b, 0, idx))
    in_specs = [
        pl.BlockSpec((1, tm, d), lambda b, t: (b, t, 0)),
        pl.BlockSpec((1, tm, dm), lambda b, t: (b, t, 0)),
        _resident((dm, d)),
        mod_spec(2),
        _resident((1, d)),
        mod_spec(3),
        mod_spec(4),
        mod_spec(5),
        _resident((d, dff)),
        _resident((dff, d)),
    ]
    args = [x, m, w_o, mod, gain_mlp.reshape(1, d), mod, mod, mod, w_up, w_down]
    if final:
        in_specs.append(_resident((1, d)))
        args.append(final_gain.reshape(1, d))
    return pl.pallas_call(
        functools.partial(_tail_kernel, ff_chunk=min(dff, 1024), final=final),
        grid=(bsz, seq // tm),
        in_specs=in_specs,
        out_specs=pl.BlockSpec((1, tm, d), lambda b, t: (b, t, 0)),
        out_shape=jax.ShapeDtypeStruct((bsz, seq, d), _F32),
        compiler_params=_cparams("parallel", "parallel"),
        name="layer_tail",
    )(*args)


def kernel(x, c, ada_w, ada_b, norm_mix, norm_mlp, mlp_up, mlp_down, moba_wqkv, moba_wo, hgrn_w_in, hgrn_lb, hgrn_norm, hgrn_wo, rg_w_in, rg_conv_w, rg_conv_b, rg_w_a, rg_b_a, rg_w_i, rg_b_i, rg_lambda, rg_wo, final_norm):
    depth = ada_w.shape[0]
    bsz, seq, d = x.shape
    tiles = _tiles(seq)
    bf = lambda w: w.astype(_BF16)

    mod_all = _ada_modulation(c, ada_w, ada_b)
    lb_all = _hgrn_lower_bounds(hgrn_lb)
    i_a = i_b = i_c = 0
    for layer in range(depth):
        mod = mod_all[layer].reshape(bsz, 1, 6 * d)
        kind = layer % N_MIXERS
        if kind == 0:
            qkv = _project(x, norm_mix[layer], mod, bf(moba_wqkv[i_a]), _BF16, tiles["proj"])
            m = _moba_attention(qkv, MOBA_HEADS, tiles["moba_heads"])
            w_o = moba_wo[i_a]
            i_a += 1
        elif kind == 1:
            m = _hgrn_mixer(x, norm_mix[layer], mod, bf(hgrn_w_in[i_b]), lb_all[layer], hgrn_norm[i_b],
                            HGRN_HEADS, tiles["hgrn"])
            w_o = hgrn_wo[i_b]
            i_b += 1
        else:
            m = _rglru_mixer(x, norm_mix[layer], mod, bf(rg_w_in[i_c]), rg_conv_w[i_c], rg_conv_b[i_c],
                             rg_w_a[i_c], rg_b_a[i_c], rg_w_i[i_c], rg_b_i[i_c], rg_lambda[i_c],
                             tiles["rglru"])
            w_o = rg_wo[i_c]
            i_c += 1
        x = _layer_tail(x, m, bf(w_o), mod, norm_mlp[layer], bf(mlp_up[layer]), bf(mlp_down[layer]),
                        final_norm if layer == depth - 1 else None, tiles["tail"])
    return x
```

```python
import functools

import jax
import jax.numpy as jnp
from jax import lax
from jax.experimental import pallas as pl
from jax.experimental.pallas import tpu as pltpu

_F32 = jnp.float32
_BF16 = jnp.bfloat16

NORM_EPS = 1e-6
N_MIXERS = 3
MOBA_HEADS = 8
MOBA_BLOCK = 256
MOBA_TOPK = 3
HGRN_HEADS = 8
HGRN_CHUNK = 64
HGRN_SUBTILE = 256
RG_BLOCKS = 4
RG_CONV_WIDTH = 4
RG_C = 8.0
RG_SUBTILE = 256
RG_SCAN_PHASES = 8
MASK_VALUE = -1e30

V7X_VMEM_LIMIT_BYTES = 56 * 1024 * 1024
LANES = 128
SUBLANES = 8


def _tiles(seq):
    return dict(proj=min(seq, 1024), tail=min(seq, 1024), hgrn=min(seq, 1024), rglru=min(seq, 1024), moba_heads=4)


def _cparams(*sem):
    return pltpu.CompilerParams(dimension_semantics=sem, vmem_limit_bytes=V7X_VMEM_LIMIT_BYTES)


def _resident(shape):
    zeros = (0,) * len(shape)
    return pl.BlockSpec(shape, lambda *_: zeros, pipeline_mode=pl.Buffered(1))


def _sigmoid(x):
    return 1.0 / (1.0 + jnp.exp(-x))


def _sigmoid_tanh(x):
    return 0.5 * jnp.tanh(0.5 * x) + 0.5


def _silu_tanh(x):
    half = 0.5 * x
    return half + half * jnp.tanh(half)


def _rms_norm(x, gain):
    return x * lax.rsqrt(jnp.mean(x * x, axis=-1, keepdims=True) + NORM_EPS) * gain


def _norm_mod(x, gain, shift, scale):
    return x * lax.rsqrt(jnp.mean(x * x, axis=-1, keepdims=True) + NORM_EPS) * (gain * (1.0 + scale)) + shift


def _dot(a, b):
    return jnp.dot(a, b, preferred_element_type=_F32)


def _dot_nt(a, b):
    return lax.dot_general(a, b, (((1,), (1,)), ((), ())), preferred_element_type=_F32)


def _ada_kernel(c_ref, w_ref, b_ref, o_ref):
    c = c_ref[...]
    cond = c * _sigmoid(c)
    bsz = cond.shape[0]
    w = w_ref[0]
    c_hi = cond.astype(_BF16)
    c_lo = (cond - c_hi.astype(_F32)).astype(_BF16)
    w_hi = w.astype(_BF16)
    w_lo = (w - w_hi.astype(_F32)).astype(_BF16)
    with_hi = _dot(jnp.concatenate([c_hi, c_lo], axis=0), w_hi)
    o_ref[0] = with_hi[:bsz] + with_hi[bsz:] + _dot(c_hi, w_lo) + b_ref[0]


def _ada_modulation(c, ada_w, ada_b):
    depth, d, n = ada_w.shape
    bsz = c.shape[0]
    tn = min(n, 1536)
    return pl.pallas_call(
        _ada_kernel,
        grid=(depth, n // tn),
        in_specs=[
            pl.BlockSpec((bsz, d), lambda l, j: (0, 0)),
            pl.BlockSpec((1, d, tn), lambda l, j: (l, 0, j)),
            pl.BlockSpec((1, 1, tn), lambda l, j: (l, 0, j)),
        ],
        out_specs=pl.BlockSpec((1, bsz, tn), lambda l, j: (l, 0, j)),
        out_shape=jax.ShapeDtypeStruct((depth, bsz, n), _F32),
        compiler_params=_cparams("arbitrary", "arbitrary"),
        name="ada_modulation",
    )(c, ada_w, ada_b.reshape(depth, 1, n))


def _lb_kernel(x_ref, o_ref):
    depth = x_ref.shape[0]
    rows = [x_ref[l:l + 1, :] for l in range(depth)]
    mx = functools.reduce(jnp.maximum, rows)
    es = [jnp.exp(r - mx) for r in rows]
    den = functools.reduce(lambda a, b: a + b, es)
    first = es[0] / den
    run = first
    o_ref[0:1, :] = run - first
    for l in range(1, depth):
        run = run + es[l] / den
        o_ref[l:l + 1, :] = run - first


def _hgrn_lower_bounds(hgrn_lb):
    return pl.pallas_call(
        _lb_kernel,
        out_shape=jax.ShapeDtypeStruct(hgrn_lb.shape, _F32),
        name="hgrn_lower_bounds",
    )(hgrn_lb.astype(_F32))


def _proj_kernel(x_ref, g_ref, sh_ref, sc_ref, w_ref, o_ref, *, n_chunk):
    h = _norm_mod(x_ref[0], g_ref[...], sh_ref[0], sc_ref[0]).astype(_BF16)
    for c in range(0, w_ref.shape[1], n_chunk):
        o_ref[0, :, c:c + n_chunk] = _dot(h, w_ref[:, c:c + n_chunk]).astype(o_ref.dtype)


def _project(x, gain, mod, w, out_dtype, tm):
    bsz, seq, d = x.shape
    n = w.shape[1]
    return pl.pallas_call(
        functools.partial(_proj_kernel, n_chunk=min(n, 1024)),
        grid=(bsz, seq // tm),
        in_specs=[
            pl.BlockSpec((1, tm, d), lambda b, t: (b, t, 0)),
            _resident((1, d)),
            pl.BlockSpec((1, 1, d), lambda b, t: (b, 0, 0)),
            pl.BlockSpec((1, 1, d), lambda b, t: (b, 0, 1)),
            _resident((d, n)),
        ],
        out_specs=pl.BlockSpec((1, tm, n), lambda b, t: (b, t, 0)),
        out_shape=jax.ShapeDtypeStruct((bsz, seq, n), out_dtype),
        compiler_params=_cparams("parallel", "parallel"),
        name="project",
    )(x, gain.reshape(1, d), mod, mod, w)


def _moba_kernel(q_ref, k_ref, v_ref, o_ref, *, blk, topk, scale):
    dh = LANES
    for hd in range(q_ref.shape[2] // dh):
        cols = slice(hd * dh, (hd + 1) * dh)
        _moba_head(q_ref[0, :, cols], k_ref[0, :, cols], v_ref[0, :, cols], o_ref, cols,
                   blk=blk, topk=topk, scale=scale)


def _moba_head(q, k, v, o_ref, cols, *, blk, topk, scale):
    seq, dh = k.shape
    nblk = seq // blk

    row = lax.broadcasted_iota(jnp.int32, (nblk, seq), 0)
    col = lax.broadcasted_iota(jnp.int32, (nblk, seq), 1)
    fully_past = (row + 1) * blk <= col
    kmean = jnp.sum(k.astype(_F32).reshape(nblk, blk, dh), axis=1) * (1.0 / blk)
    kmean_hi = kmean.astype(_BF16)
    kmean_lo = (kmean - kmean_hi.astype(_F32)).astype(_BF16)
    gate_parts = _dot_nt(jnp.concatenate([kmean_hi, kmean_lo], axis=0), q)
    gate_t = gate_parts[:nblk, :] + gate_parts[nblk:, :]

    bias_rows = []
    for j in range(nblk):
        g_j = gate_t[j:j + 1, :]
        beats = jnp.logical_and(
            fully_past, jnp.logical_or(gate_t > g_j, jnp.logical_and(gate_t == g_j, row < j)))
        rank = jnp.sum(beats.astype(_F32), axis=0, keepdims=True)
        masked = jnp.logical_and(fully_past[j:j + 1, :], rank >= topk)
        bias_rows.append(jnp.where(masked, MASK_VALUE, 0.0))
    bias_t = jnp.concatenate(bias_rows + [jnp.zeros((dh - nblk, seq), _F32)], axis=0)

    key_pos = lax.broadcasted_iota(jnp.int32, (seq, dh), 0)
    lane = lax.broadcasted_iota(jnp.int32, (seq, dh), 1)
    onehot = jnp.logical_and(lane * blk <= key_pos, key_pos < (lane + 1) * blk).astype(_BF16)
    k_exp2 = (k.astype(_F32) * (scale * 1.4426950408889634)).astype(_BF16)
    k_aug = jnp.concatenate([k_exp2, onehot], axis=1)
    v_aug = jnp.concatenate([v, jnp.ones((seq, dh), _BF16)], axis=1)

    qpos = lax.broadcasted_iota(jnp.int32, (blk, blk), 0)
    kpos = lax.broadcasted_iota(jnp.int32, (blk, blk), 1)
    causal = kpos <= qpos
    order = [0] + list(range(nblk - 1, 0, -1))
    logits = {}
    for i in order:
        lo, hi = i * blk, (i + 1) * blk
        if i == 0:
            bias = jnp.zeros((blk, dh), _BF16)
        else:
            bias = jnp.transpose(bias_t[:, lo:hi]).astype(_BF16)
        q_aug = jnp.concatenate([q[lo:hi, :], bias], axis=1)
        logits[i] = _dot_nt(q_aug, k_aug[:hi, :]).astype(_BF16)
    probs = {}
    for i in order:
        lo, hi = i * blk, (i + 1) * blk
        s = logits[i]
        s_own = jnp.where(causal, s[:, lo:hi], jnp.asarray(MASK_VALUE, _BF16))
        s = s_own if i == 0 else jnp.concatenate([s[:, :lo], s_own], axis=1)
        m = jnp.max(s, axis=-1, keepdims=True)
        probs[i] = jnp.exp2(s - m)
    for i in order:
        lo, hi = i * blk, (i + 1) * blk
        o = _dot(probs[i], v_aug[:hi, :])
        o_ref[0, lo:hi, cols] = (o[:, :dh] * (1.0 / o[:, dh:dh + 1])).astype(o_ref.dtype)


def _moba_attention(qkv, heads, heads_per_step):
    bsz, seq, three_hd = qkv.shape
    dh = three_hd // (3 * heads)
    blk = MOBA_BLOCK
    assert seq % blk == 0 and dh == LANES and heads % heads_per_step == 0
    nblk = seq // blk
    topk = min(MOBA_TOPK, nblk - 1)
    groups = heads // heads_per_step
    width = heads_per_step * dh
    kern = functools.partial(_moba_kernel, blk=blk, topk=topk, scale=dh ** -0.5)
    return pl.pallas_call(
        kern,
        grid=(bsz, groups),
        in_specs=[
            pl.BlockSpec((1, seq, width), lambda b, g: (b, 0, g)),
            pl.BlockSpec((1, seq, width), lambda b, g: (b, 0, groups + g)),
            pl.BlockSpec((1, seq, width), lambda b, g: (b, 0, 2 * groups + g)),
        ],
        out_specs=pl.BlockSpec((1, seq, width), lambda b, g: (b, 0, g)),
        out_shape=jax.ShapeDtypeStruct((bsz, seq, heads * dh), _BF16),
        compiler_params=_cparams("parallel", "parallel"),
        name="moba_attention",
    )(qkv, qkv, qkv)


def _split2(x):
    hi = x.astype(_BF16)
    return hi, (x - hi.astype(_F32)).astype(_BF16)


def _hgrn_kernel(x_ref, gn_ref, sh_ref, sc_ref, w_ref, lb_ref, gain_ref, o_ref, st_ref, *, heads, chunk, sub):
    @pl.when(pl.program_id(1) == 0)
    def _init():
        st_ref[...] = jnp.zeros_like(st_ref)

    fdim = lb_ref.shape[1]
    hv = heads * gain_ref.shape[1]
    tiles = [slice(r, r + sub) for r in range(0, x_ref.shape[1], sub)]

    def project(rows):
        h = _norm_mod(x_ref[0, rows, :], gn_ref[...], sh_ref[0], sc_ref[0]).astype(_BF16)
        return (_dot(h, w_ref[:, :fdim]), _dot(h, w_ref[:, fdim:2 * fdim]),
                _dot(h, w_ref[:, 2 * fdim:2 * fdim + hv]).astype(_BF16), _dot(h, w_ref[:, 2 * fdim + hv:]))

    z = project(tiles[0])
    for i, rows in enumerate(tiles):
        gates = _hgrn_gates(z[0], z[1], lb_ref[...], chunk=chunk)
        v, zg = z[2], z[3]
        if i + 1 < len(tiles):
            z = project(tiles[i + 1])
        _hgrn_heads(gates, v, zg, gain_ref[...], o_ref, rows, st_ref, heads=heads, chunk=chunk)


def _hgrn_gates(zq, zf, lb, *, chunk):
    tm, fdim = zq.shape
    nck = tm // chunk
    q = _silu_tanh(zq)
    fgate = lb + (1.0 - lb) * _sigmoid(zf)
    k = 1.0 - fgate

    shift = chunk.bit_length() - 1
    r_i = lax.broadcasted_iota(jnp.int32, (tm, tm), 0)
    c_i = lax.broadcasted_iota(jnp.int32, (tm, tm), 1)
    block_causal = jnp.logical_and(jnp.right_shift(r_i, shift) == jnp.right_shift(c_i, shift), c_i <= r_i)
    tri = block_causal.astype(_BF16)
    b = functools.reduce(lambda a, c: a + c, [_dot(tri, part) for part in _split2(jnp.log(fgate))])

    by_chunk = lambda t: t.reshape(nck, chunk, fdim)
    b3, q3, k3 = by_chunk(b), by_chunk(q), by_chunk(k)
    b_mid = b3[:, chunk // 2:chunk // 2 + 1, :]
    b_last = b3[:, chunk - 1:chunk, :]
    flat = lambda t: t.astype(_BF16).reshape(tm, fdim)
    qd = flat(q3 * jnp.exp(b3 - b_mid))
    kd = flat(k3 * jnp.exp(b_mid - b3))
    q_in = flat(q3 * jnp.exp(b3))
    k_out = flat(k3 * jnp.exp(b_last - b3))
    decay = jnp.exp(b_last)
    return qd, kd, q_in, k_out, decay, block_causal


def _hgrn_heads(gates, v, zg, gain, o_ref, rows_out, st_ref, *, heads, chunk):
    qd, kd, q_in, k_out, decay, block_causal = gates
    tm, fdim = qd.shape
    nck = tm // chunk
    kdim = fdim // heads
    vdim = gain.shape[1]
    ksl = [slice(hd * kdim, (hd + 1) * kdim) for hd in range(heads)]
    vsl = [slice(hd * vdim, (hd + 1) * vdim) for hd in range(heads)]
    crow = [slice(c * chunk, (c + 1) * chunk) for c in range(nck)]
    causal = block_causal[:chunk, :chunk]
    a_all = [[jnp.where(causal, _dot_nt(qd[rows, ks], kd[rows, ks]), 0.0).astype(_BF16) for rows in crow]
             for ks in ksl]
    v_t = [[jnp.transpose(v[rows, vs].astype(_F32)).astype(_BF16) for rows in crow] for vs in vsl]
    upd = [[_dot(v_t[hd][c], k_out[crow[c], ksl[hd]]) for c in range(nck)] for hd in range(heads)]
    for hd in range(heads):
        st = st_ref[hd]
        outs = []
        for c in range(nck):
            lhs = jnp.concatenate([q_in[crow[c], ksl[hd]], a_all[hd][c]], axis=1)
            rhs_t = jnp.concatenate([st.astype(_BF16), v_t[hd][c]], axis=1)
            outs.append(_dot_nt(lhs, rhs_t))
            st = decay[c, :, ksl[hd]] * st + upd[hd][c]
        st_ref[hd] = st
        o = jnp.concatenate(outs, axis=0)
        zg_h = zg[:, vsl[hd]]
        o_ref[0, rows_out, vsl[hd]] = (_rms_norm(o, gain) * _silu_tanh(zg_h)).astype(o_ref.dtype)


def _hgrn_mixer(x, gain_mix, mod, w_in, lb, g_gain, heads, tm):
    bsz, seq, d = x.shape
    n = w_in.shape[1]
    fdim = lb.shape[-1]
    vdim = g_gain.shape[-1]
    kdim = fdim // heads
    sub = min(tm, HGRN_SUBTILE)
    assert kdim == LANES and vdim == LANES and n == 2 * fdim + 2 * heads * vdim
    assert HGRN_CHUNK & (HGRN_CHUNK - 1) == 0 and sub % HGRN_CHUNK == 0 and tm % sub == 0
    return pl.pallas_call(
        functools.partial(_hgrn_kernel, heads=heads, chunk=HGRN_CHUNK, sub=sub),
        grid=(bsz, seq // tm),
        in_specs=[
            pl.BlockSpec((1, tm, d), lambda b, t: (b, t, 0)),
            _resident((1, d)),
            pl.BlockSpec((1, 1, d), lambda b, t: (b, 0, 0)),
            pl.BlockSpec((1, 1, d), lambda b, t: (b, 0, 1)),
            _resident((d, n)),
            _resident((1, fdim)),
            _resident((1, vdim)),
        ],
        out_specs=pl.BlockSpec((1, tm, heads * vdim), lambda b, t: (b, t, 0)),
        out_shape=jax.ShapeDtypeStruct((bsz, seq, heads * vdim), _BF16),
        scratch_shapes=[pltpu.VMEM((heads, vdim, kdim), _F32)],
        compiler_params=_cparams("parallel", "arbitrary"),
        name="hgrn2_mixer",
    )(x, gain_mix.reshape(1, d), mod, mod, w_in, lb.reshape(1, fdim), g_gain.reshape(1, vdim))


def _gelu_tanh(x):
    half = 0.5 * x
    return half + half * jnp.tanh(x * (0.7978845608028654 + 0.035677408136300125 * (x * x)))


def _rglru_kernel(x_ref, gn_ref, sh_ref, sc_ref, w_ref, cw_ref, cb_ref, wa_ref, ba_ref, wi_ref, bi_ref,
                  lam_ref, o_ref, hc_ref, xh_ref, *, phases, sub):
    t = pl.program_id(1)

    @pl.when(t == 0)
    def _init():
        hc_ref[...] = jnp.zeros_like(hc_ref)
        xh_ref[...] = jnp.zeros_like(xh_ref)

    width = o_ref.shape[2]
    ng = sub // phases
    shift = ng.bit_length() - 1
    i_r = lax.broadcasted_iota(jnp.int32, (sub, sub), 0)
    i_c = lax.broadcasted_iota(jnp.int32, (sub, sub), 1)
    time_of = lambda i: jnp.bitwise_and(i, ng - 1) * phases + jnp.right_shift(i, shift)
    perm = (i_c == time_of(i_r)).astype(_BF16)
    unperm = (i_r == time_of(i_c)).astype(_BF16)
    slab = lambda val, p: val[p * ng:(p + 1) * ng, :]
    grow = lax.broadcasted_iota(jnp.int32, (ng, 1), 0)
    row = lax.broadcasted_iota(jnp.int32, (sub, 1), 0)
    taps = cw_ref.shape[0]
    nb = wa_ref.shape[0]
    bw = width // nb
    neg_lam = -lam_ref[...]
    softplus = jnp.maximum(neg_lam, 0.0) + jnp.log1p(jnp.exp(-jnp.abs(neg_lam)))
    log_a_per_r = -RG_C * softplus

    def project(rows):
        h = _norm_mod(x_ref[0, rows, :], gn_ref[...], sh_ref[0], sc_ref[0]).astype(_BF16)
        h = _dot(perm, h).astype(_BF16)
        return _dot(h, w_ref[:, :width]), _dot(h, w_ref[:, width:])

    tiles = [slice(r, r + sub) for r in range(0, x_ref.shape[1], sub)]
    z = project(tiles[0])
    for i, rows in enumerate(tiles):
        y_pre, x_br = z
        prev_group = {p: jnp.where(grow >= 1, pltpu.roll(slab(x_br, p), 1, 0), xh_ref[p:p + 1, :])
                      for p in range(phases - (taps - 1), phases)}
        for p in range(phases - (taps - 1), phases):
            xh_ref[p:p + 1, :] = slab(x_br, p)[ng - 1:ng, :]
        conv_slabs = []
        for p in range(phases):
            acc = cb_ref[...]
            for j in range(taps):
                src = p - (taps - 1) + j
                acc = acc + (slab(x_br, src) if src >= 0 else prev_group[src + phases]) * cw_ref[j:j + 1, :]
            conv_slabs.append(acc)
        x_conv = jnp.concatenate(conv_slabs, axis=0)
        if i + 1 < len(tiles):
            z = project(tiles[i + 1])

        xcb = x_conv.astype(_BF16)
        r = jnp.concatenate([_dot(xcb[:, n * bw:(n + 1) * bw], wa_ref[n]) for n in range(nb)], axis=1)
        gi = jnp.concatenate([_dot(xcb[:, n * bw:(n + 1) * bw], wi_ref[n]) for n in range(nb)], axis=1)
        r = _sigmoid_tanh(r + ba_ref[...])
        gi = _sigmoid_tanh(gi + bi_ref[...])
        log_a = r * log_a_per_r
        a = jnp.exp(log_a)
        th = jnp.tanh(log_a)
        mult = jnp.sqrt(-2.0 * th) * lax.rsqrt(1.0 - th)
        if i == 0:
            mult = jnp.where(jnp.logical_and(row == 0, t == 0), 1.0, mult)
        u = (gi * x_conv) * mult

        h_loc, a_cum = [slab(u, 0)], [slab(a, 0)]
        for p in range(1, phases):
            h_loc.append(slab(a, p) * h_loc[-1] + slab(u, p))
            a_cum.append(slab(a, p) * a_cum[-1])
        ga, gu = a_cum[-1], h_loc[-1]
        d = 1
        while d < ng:
            keep = grow >= d
            a_sh = jnp.where(keep, pltpu.roll(ga, d, 0), 1.0)
            u_sh = jnp.where(keep, pltpu.roll(gu, d, 0), 0.0)
            ga, gu = ga * a_sh, gu + ga * u_sh
            d *= 2
        carry = hc_ref[...]
        h_end = gu + ga * carry
        hc_ref[...] = h_end[ng - 1:ng, :]
        h_in = jnp.where(grow >= 1, pltpu.roll(h_end, 1, 0), carry)
        hs = jnp.concatenate([h_loc[p] + a_cum[p] * h_in for p in range(phases)], axis=0)
        out = (hs * _gelu_tanh(y_pre)).astype(_BF16)
        o_ref[0, rows, :] = _dot(unperm, out).astype(o_ref.dtype)


def _rglru_mixer(x, gain_mix, mod, w_in, conv_w, conv_b, w_a, b_a, w_i, b_i, lam, tm):
    bsz, seq, d = x.shape
    two_w = w_in.shape[1]
    width = two_w // 2
    nb, bw, _ = w_a.shape
    phases = RG_SCAN_PHASES
    sub = min(tm, RG_SUBTILE)
    ng = sub // phases
    assert tm % sub == 0 and sub % phases == 0 and ng % SUBLANES == 0 and ng & (ng - 1) == 0
    assert phases >= RG_CONV_WIDTH
    row = lambda v: v.reshape(1, width)
    return pl.pallas_call(
        functools.partial(_rglru_kernel, phases=phases, sub=sub),
        grid=(bsz, seq // tm),
        in_specs=[
            pl.BlockSpec((1, tm, d), lambda b, t: (b, t, 0)),
            _resident((1, d)),
            pl.BlockSpec((1, 1, d), lambda b, t: (b, 0, 0)),
            pl.BlockSpec((1, 1, d), lambda b, t: (b, 0, 1)),
            _resident((d, two_w)),
            _resident((RG_CONV_WIDTH, width)),
            _resident((1, width)),
            _resident((nb, bw, bw)),
            _resident((1, width)),
            _resident((nb, bw, bw)),
            _resident((1, width)),
            _resident((1, width)),
        ],
        out_specs=pl.BlockSpec((1, tm, width), lambda b, t: (b, t, 0)),
        out_shape=jax.ShapeDtypeStruct((bsz, seq, width), _BF16),
        scratch_shapes=[
            pltpu.VMEM((1, width), _F32),
            pltpu.VMEM((phases, width), _F32),
        ],
        compiler_params=_cparams("parallel", "arbitrary"),
        name="rglru_mixer",
    )(x, gain_mix.reshape(1, d), mod, mod, w_in, conv_w, row(conv_b), w_a.astype(_BF16), row(b_a),
      w_i.astype(_BF16), row(b_i), row(lam))


def _tail_kernel(x_ref, m_ref, wo_ref, g1_ref, gn_ref, sh_ref, sc_ref, g2_ref, wup_ref, wdn_ref,
                 *rest, ff_chunk, final):
    o_ref = rest[-1]
    x1 = x_ref[0] + g1_ref[0] * _dot(m_ref[0], wo_ref[...])
    h = _norm_mod(x1, gn_ref[...], sh_ref[0], sc_ref[0]).astype(_BF16)
    acc = jnp.zeros_like(x1)
    for c in range(0, wup_ref.shape[1], ff_chunk):
        u = jnp.maximum(_dot(h, wup_ref[:, c:c + ff_chunk]), 0.0)
        acc = acc + _dot((u * u).astype(_BF16), wdn_ref[c:c + ff_chunk, :])
    x2 = x1 + g2_ref[0] * acc
    if final:
        x2 = _rms_norm(x2, rest[0][...])
    o_ref[0] = x2


def _layer_tail(x, m, w_o, mod, gain_mlp, w_up, w_down, final_gain, tm):
    bsz, seq, d = x.shape
    dm = m.shape[-1]
    dff = w_up.shape[1]
    final = final_gain is not None
    mod_spec = lambda idx: pl.BlockSpec((1, 1, d), lambda b, t: (b, 0, idx))
    in_specs = [
        pl.BlockSpec((1, tm, d), lambda b, t: (b, t, 0)),
        pl.BlockSpec((1, tm, dm), lambda b, t: (b, t, 0)),
        _resident((dm, d)),
        mod_spec(2),
        _resident((1, d)),
        mod_spec(3),
        mod_spec(4),
        mod_spec(5),
        _resident((d, dff)),
        _resident((dff, d)),
    ]
    args = [x, m, w_o, mod, gain_mlp.reshape(1, d), mod, mod, mod, w_up, w_down]
    if final:
        in_specs.append(_resident((1, d)))
        args.append(final_gain.reshape(1, d))
    return pl.pallas_call(
        functools.partial(_tail_kernel, ff_chunk=min(dff, 1024), final=final),
        grid=(bsz, seq // tm),
        in_specs=in_specs,
        out_specs=pl.BlockSpec((1, tm, d), lambda b, t: (b, t, 0)),
        out_shape=jax.ShapeDtypeStruct((bsz, seq, d), _F32),
        compiler_params=_cparams("parallel", "parallel"),
        name="layer_tail",
    )(*args)


def kernel(x, c, ada_w, ada_b, norm_mix, norm_mlp, mlp_up, mlp_down, moba_wqkv, moba_wo, hgrn_w_in, hgrn_lb, hgrn_norm, hgrn_wo, rg_w_in, rg_conv_w, rg_conv_b, rg_w_a, rg_b_a, rg_w_i, rg_b_i, rg_lambda, rg_wo, final_norm):
    depth = ada_w.shape[0]
    bsz, seq, d = x.shape
    tiles = _tiles(seq)
    bf = lambda w: w.astype(_BF16)

    mod_all = _ada_modulation(c, ada_w, ada_b)
    lb_all = _hgrn_lower_bounds(hgrn_lb)
    i_a = i_b = i_c = 0
    for layer in range(depth):
        mod = mod_all[layer].reshape(bsz, 1, 6 * d)
        kind = layer % N_MIXERS
        if kind == 0:
            qkv = _project(x, norm_mix[layer], mod, bf(moba_wqkv[i_a]), _BF16, tiles["proj"])
            m = _moba_attention(qkv, MOBA_HEADS, tiles["moba_heads"])
            w_o = moba_wo[i_a]
            i_a += 1
        elif kind == 1:
            m = _hgrn_mixer(x, norm_mix[layer], mod, bf(hgrn_w_in[i_b]), lb_all[layer], hgrn_norm[i_b],
                            HGRN_HEADS, tiles["hgrn"])
            w_o = hgrn_wo[i_b]
            i_b += 1
        else:
            m = _rglru_mixer(x, norm_mix[layer], mod, bf(rg_w_in[i_c]), rg_conv_w[i_c], rg_conv_b[i_c],
                             rg_w_a[i_c], rg_b_a[i_c], rg_w_i[i_c], rg_b_i[i_c], rg_lambda[i_c],
                             tiles["rglru"])
            w_o = rg_wo[i_c]
            i_c += 1
        x = _layer_tail(x, m, bf(w_o), mod, norm_mlp[layer], bf(mlp_up[layer]), bf(mlp_down[layer]),
                        final_norm if layer == depth - 1 else None, tiles["tail"])
    return x
```

```python
import functools

import jax
import jax.numpy as jnp
from jax import lax
from jax.experimental import pallas as pl
from jax.experimental.pallas import tpu as pltpu

_F32 = jnp.float32
_BF16 = jnp.bfloat16

NORM_EPS = 1e-6
N_MIXERS = 3
MOBA_HEADS = 8
MOBA_BLOCK = 256
MOBA_TOPK = 3
HGRN_HEADS = 8
HGRN_CHUNK = 64
HGRN_SUBTILE = 256
RG_BLOCKS = 4
RG_CONV_WIDTH = 4
RG_C = 8.0
RG_SUBTILE = 256
RG_SCAN_PHASES = 8
MASK_VALUE = -1e30

V7X_VMEM_LIMIT_BYTES = 56 * 1024 * 1024
LANES = 128
SUBLANES = 8


def _tiles(seq):
    return dict(proj=min(seq, 1024), tail=min(seq, 1024), hgrn=min(seq, 1024), rglru=min(seq, 1024), moba_heads=4)


def _cparams(*sem):
    return pltpu.CompilerParams(dimension_semantics=sem, vmem_limit_bytes=V7X_VMEM_LIMIT_BYTES)


def _resident(shape):
    zeros = (0,) * len(shape)
    return pl.BlockSpec(shape, lambda *_: zeros, pipeline_mode=pl.Buffered(1))


def _resident_layer(stack, index):
    return pl.BlockSpec((None,) + stack.shape[1:], lambda *_: (index, 0, 0), pipeline_mode=pl.Buffered(1))


def _sigmoid(x):
    return 1.0 / (1.0 + jnp.exp(-x))


def _sigmoid_tanh(x):
    return 0.5 * jnp.tanh(0.5 * x) + 0.5


def _silu_tanh(x):
    half = 0.5 * x
    return half + half * jnp.tanh(half)


def _rms_norm(x, gain):
    return x * lax.rsqrt(jnp.mean(x * x, axis=-1, keepdims=True) + NORM_EPS) * gain


def _norm_mod(x, gain, shift, scale):
    return x * lax.rsqrt(jnp.mean(x * x, axis=-1, keepdims=True) + NORM_EPS) * (gain * (1.0 + scale)) + shift


def _dot(a, b):
    return jnp.dot(a, b, preferred_element_type=_F32)


def _dot_nt(a, b):
    return lax.dot_general(a, b, (((1,), (1,)), ((), ())), preferred_element_type=_F32)


def _ada_kernel(c_ref, w_ref, b_ref, o_ref):
    c = c_ref[...]
    cond = c * _sigmoid(c)
    bsz = cond.shape[0]
    w = w_ref[0]
    c_hi = cond.astype(_BF16)
    c_lo = (cond - c_hi.astype(_F32)).astype(_BF16)
    w_hi = w.astype(_BF16)
    w_lo = (w - w_hi.astype(_F32)).astype(_BF16)
    with_hi = _dot(jnp.concatenate([c_hi, c_lo], axis=0), w_hi)
    o_ref[0] = with_hi[:bsz] + with_hi[bsz:] + _dot(c_hi, w_lo) + b_ref[0]


def _ada_modulation(c, ada_w, ada_b):
    depth, d, n = ada_w.shape
    bsz = c.shape[0]
    tn = min(n, 1536)
    return pl.pallas_call(
        _ada_kernel,
        grid=(depth, n // tn),
        in_specs=[
            pl.BlockSpec((bsz, d), lambda l, j: (0, 0)),
            pl.BlockSpec((1, d, tn), lambda l, j: (l, 0, j)),
            pl.BlockSpec((1, 1, tn), lambda l, j: (l, 0, j)),
        ],
        out_specs=pl.BlockSpec((1, bsz, tn), lambda l, j: (l, 0, j)),
        out_shape=jax.ShapeDtypeStruct((depth, bsz, n), _F32),
        compiler_params=_cparams("arbitrary", "arbitrary"),
        name="ada_modulation",
    )(c, ada_w, ada_b.reshape(depth, 1, n))


def _lb_kernel(x_ref, o_ref):
    depth = x_ref.shape[0]
    rows = [x_ref[l:l + 1, :] for l in range(depth)]
    mx = functools.reduce(jnp.maximum, rows)
    es = [jnp.exp(r - mx) for r in rows]
    den = functools.reduce(lambda a, b: a + b, es)
    first = es[0] / den
    run = first
    o_ref[0:1, :] = run - first
    for l in range(1, depth):
        run = run + es[l] / den
        o_ref[l:l + 1, :] = run - first


def _hgrn_lower_bounds(hgrn_lb):
    return pl.pallas_call(
        _lb_kernel,
        out_shape=jax.ShapeDtypeStruct(hgrn_lb.shape, _F32),
        name="hgrn_lower_bounds",
    )(hgrn_lb.astype(_F32))


def _proj_kernel(x_ref, g_ref, sh_ref, sc_ref, w_ref, o_ref, *, n_chunk):
    h = _norm_mod(x_ref[0], g_ref[...], sh_ref[0], sc_ref[0]).astype(_BF16)
    for c in range(0, w_ref.shape[1], n_chunk):
        o_ref[0, :, c:c + n_chunk] = _dot(h, w_ref[:, c:c + n_chunk]).astype(o_ref.dtype)


def _project(x, gain, mod, w_stack, index, out_dtype, tm):
    bsz, seq, d = x.shape
    n = w_stack.shape[2]
    return pl.pallas_call(
        functools.partial(_proj_kernel, n_chunk=min(n, 1024)),
        grid=(bsz, seq // tm),
        in_specs=[
            pl.BlockSpec((1, tm, d), lambda b, t: (b, t, 0)),
            _resident((1, d)),
            pl.BlockSpec((1, 1, d), lambda b, t: (b, 0, 0)),
            pl.BlockSpec((1, 1, d), lambda b, t: (b, 0, 1)),
            _resident_layer(w_stack, index),
        ],
        out_specs=pl.BlockSpec((1, tm, n), lambda b, t: (b, t, 0)),
        out_shape=jax.ShapeDtypeStruct((bsz, seq, n), out_dtype),
        compiler_params=_cparams("parallel", "parallel"),
        name="project",
    )(x, gain.reshape(1, d), mod, mod, w_stack)


def _moba_kernel(q_ref, k_ref, v_ref, o_ref, *, blk, topk, scale):
    dh = LANES
    for hd in range(q_ref.shape[2] // dh):
        cols = slice(hd * dh, (hd + 1) * dh)
        _moba_head(q_ref[0, :, cols], k_ref[0, :, cols], v_ref[0, :, cols], o_ref, cols,
                   blk=blk, topk=topk, scale=scale)


def _moba_head(q, k, v, o_ref, cols, *, blk, topk, scale):
    seq, dh = k.shape
    nblk = seq // blk

    row = lax.broadcasted_iota(jnp.int32, (nblk, seq), 0)
    col = lax.broadcasted_iota(jnp.int32, (nblk, seq), 1)
    fully_past = (row + 1) * blk <= col
    kmean = jnp.sum(k.astype(_F32).reshape(nblk, blk, dh), axis=1) * (1.0 / blk)
    kmean_hi = kmean.astype(_BF16)
    kmean_lo = (kmean - kmean_hi.astype(_F32)).astype(_BF16)
    gate_parts = _dot_nt(jnp.concatenate([kmean_hi, kmean_lo], axis=0), q)
    gate_t = gate_parts[:nblk, :] + gate_parts[nblk:, :]

    bias_rows = []
    for j in range(nblk):
        g_j = gate_t[j:j + 1, :]
        beats = jnp.logical_and(
            fully_past, jnp.logical_or(gate_t > g_j, jnp.logical_and(gate_t == g_j, row < j)))
        rank = jnp.sum(beats.astype(_F32), axis=0, keepdims=True)
        masked = jnp.logical_and(fully_past[j:j + 1, :], rank >= topk)
        bias_rows.append(jnp.where(masked, MASK_VALUE, 0.0))
    bias_t = jnp.concatenate(bias_rows + [jnp.zeros((dh - nblk, seq), _F32)], axis=0)

    key_pos = lax.broadcasted_iota(jnp.int32, (seq, dh), 0)
    lane = lax.broadcasted_iota(jnp.int32, (seq, dh), 1)
    onehot = jnp.logical_and(lane * blk <= key_pos, key_pos < (lane + 1) * blk).astype(_BF16)
    k_exp2 = (k.astype(_F32) * (scale * 1.4426950408889634)).astype(_BF16)
    k_aug = jnp.concatenate([k_exp2, onehot], axis=1)
    v_aug = jnp.concatenate([v, jnp.ones((seq, dh), _BF16)], axis=1)

    qpos = lax.broadcasted_iota(jnp.int32, (blk, blk), 0)
    kpos = lax.broadcasted_iota(jnp.int32, (blk, blk), 1)
    causal = kpos <= qpos
    order = [0] + list(range(nblk - 1, 0, -1))
    logits = {}
    for i in order:
        lo, hi = i * blk, (i + 1) * blk
        if i == 0:
            bias = jnp.zeros((blk, dh), _BF16)
        else:
            bias = jnp.transpose(bias_t[:, lo:hi]).astype(_BF16)
        q_aug = jnp.concatenate([q[lo:hi, :], bias], axis=1)
        logits[i] = _dot_nt(q_aug, k_aug[:hi, :]).astype(_BF16)
    probs = {}
    for i in order:
        lo, hi = i * blk, (i + 1) * blk
        s = logits[i]
        s_own = jnp.where(causal, s[:, lo:hi], jnp.asarray(MASK_VALUE, _BF16))
        s = s_own if i == 0 else jnp.concatenate([s[:, :lo], s_own], axis=1)
        m = jnp.max(s, axis=-1, keepdims=True)
        probs[i] = jnp.exp2(s - m)
    for i in order:
        lo, hi = i * blk, (i + 1) * blk
        o = _dot(probs[i], v_aug[:hi, :])
        o_ref[0, lo:hi, cols] = (o[:, :dh] * (1.0 / o[:, dh:dh + 1])).astype(o_ref.dtype)


def _moba_attention(qkv, heads, heads_per_step):
    bsz, seq, three_hd = qkv.shape
    dh = three_hd // (3 * heads)
    blk = MOBA_BLOCK
    assert seq % blk == 0 and dh == LANES and heads % heads_per_step == 0
    nblk = seq // blk
    topk = min(MOBA_TOPK, nblk - 1)
    groups = heads // heads_per_step
    width = heads_per_step * dh
    kern = functools.partial(_moba_kernel, blk=blk, topk=topk, scale=dh ** -0.5)
    return pl.pallas_call(
        kern,
        grid=(bsz, groups),
        in_specs=[
            pl.BlockSpec((1, seq, width), lambda b, g: (b, 0, g)),
            pl.BlockSpec((1, seq, width), lambda b, g: (b, 0, groups + g)),
            pl.BlockSpec((1, seq, width), lambda b, g: (b, 0, 2 * groups + g)),
        ],
        out_specs=pl.BlockSpec((1, seq, width), lambda b, g: (b, 0, g)),
        out_shape=jax.ShapeDtypeStruct((bsz, seq, heads * dh), _BF16),
        compiler_params=_cparams("parallel", "parallel"),
        name="moba_attention",
    )(qkv, qkv, qkv)


def _split2(x):
    hi = x.astype(_BF16)
    return hi, (x - hi.astype(_F32)).astype(_BF16)


def _hgrn_kernel(x_ref, gn_ref, sh_ref, sc_ref, w_ref, lb_ref, gain_ref, o_ref, st_ref, *, heads, chunk, sub):
    @pl.when(pl.program_id(1) == 0)
    def _init():
        st_ref[...] = jnp.zeros_like(st_ref)

    fdim = lb_ref.shape[1]
    hv = heads * gain_ref.shape[1]
    tiles = [slice(r, r + sub) for r in range(0, x_ref.shape[1], sub)]

    def project(rows):
        h = _norm_mod(x_ref[0, rows, :], gn_ref[...], sh_ref[0], sc_ref[0]).astype(_BF16)
        return (_dot(h, w_ref[:, :fdim]), _dot(h, w_ref[:, fdim:2 * fdim]),
                _dot(h, w_ref[:, 2 * fdim:2 * fdim + hv]).astype(_BF16), _dot(h, w_ref[:, 2 * fdim + hv:]))

    z = project(tiles[0])
    for i, rows in enumerate(tiles):
        gates = _hgrn_gates(z[0], z[1], lb_ref[...], chunk=chunk)
        v, zg = z[2], z[3]
        if i + 1 < len(tiles):
            z = project(tiles[i + 1])
        _hgrn_heads(gates, v, zg, gain_ref[...], o_ref, rows, st_ref, heads=heads, chunk=chunk)


def _hgrn_gates(zq, zf, lb, *, chunk):
    tm, fdim = zq.shape
    nck = tm // chunk
    q = _silu_tanh(zq)
    fgate = lb + (1.0 - lb) * _sigmoid(zf)
    k = 1.0 - fgate

    shift = chunk.bit_length() - 1
    r_i = lax.broadcasted_iota(jnp.int32, (tm, tm), 0)
    c_i = lax.broadcasted_iota(jnp.int32, (tm, tm), 1)
    block_causal = jnp.logical_and(jnp.right_shift(r_i, shift) == jnp.right_shift(c_i, shift), c_i <= r_i)
    tri = block_causal.astype(_BF16)
    b = functools.reduce(lambda a, c: a + c, [_dot(tri, part) for part in _split2(jnp.log(fgate))])

    by_chunk = lambda t: t.reshape(nck, chunk, fdim)
    b3, q3, k3 = by_chunk(b), by_chunk(q), by_chunk(k)
    b_mid = b3[:, chunk // 2:chunk // 2 + 1, :]
    b_last = b3[:, chunk - 1:chunk, :]
    flat = lambda t: t.astype(_BF16).reshape(tm, fdim)
    qd = flat(q3 * jnp.exp(b3 - b_mid))
    kd = flat(k3 * jnp.exp(b_mid - b3))
    q_in = flat(q3 * jnp.exp(b3))
    k_out = flat(k3 * jnp.exp(b_last - b3))
    decay = jnp.exp(b_last)
    return qd, kd, q_in, k_out, decay, block_causal


def _hgrn_heads(gates, v, zg, gain, o_ref, rows_out, st_ref, *, heads, chunk):
    qd, kd, q_in, k_out, decay, block_causal = gates
    tm, fdim = qd.shape
    nck = tm // chunk
    kdim = fdim // heads
    vdim = gain.shape[1]
    ksl = [slice(hd * kdim, (hd + 1) * kdim) for hd in range(heads)]
    vsl = [slice(hd * vdim, (hd + 1) * vdim) for hd in range(heads)]
    crow = [slice(c * chunk, (c + 1) * chunk) for c in range(nck)]
    causal = block_causal[:chunk, :chunk]
    a_all = [[jnp.where(causal, _dot_nt(qd[rows, ks], kd[rows, ks]), 0.0).astype(_BF16) for rows in crow]
             for ks in ksl]
    v_t = [[jnp.transpose(v[rows, vs].astype(_F32)).astype(_BF16) for rows in crow] for vs in vsl]
    upd = [[_dot(v_t[hd][c], k_out[crow[c], ksl[hd]]) for c in range(nck)] for hd in range(heads)]
    for hd in range(heads):
        st = st_ref[hd]
        outs = []
        for c in range(nck):
            lhs = jnp.concatenate([q_in[crow[c], ksl[hd]], a_all[hd][c]], axis=1)
            rhs_t = jnp.concatenate([st.astype(_BF16), v_t[hd][c]], axis=1)
            outs.append(_dot_nt(lhs, rhs_t))
            st = decay[c, :, ksl[hd]] * st + upd[hd][c]
        st_ref[hd] = st
        o = jnp.concatenate(outs, axis=0)
        zg_h = zg[:, vsl[hd]]
        o_ref[0, rows_out, vsl[hd]] = (_rms_norm(o, gain) * _silu_tanh(zg_h)).astype(o_ref.dtype)


def _hgrn_mixer(x, gain_mix, mod, w_in_stack, index, lb, g_gain, heads, tm):
    bsz, seq, d = x.shape
    n = w_in_stack.shape[2]
    fdim = lb.shape[-1]
    vdim = g_gain.shape[-1]
    kdim = fdim // heads
    sub = min(tm, HGRN_SUBTILE)
    assert kdim == LANES and vdim == LANES and n == 2 * fdim + 2 * heads * vdim
    assert HGRN_CHUNK & (HGRN_CHUNK - 1) == 0 and sub % HGRN_CHUNK == 0 and tm % sub == 0
    return pl.pallas_call(
        functools.partial(_hgrn_kernel, heads=heads, chunk=HGRN_CHUNK, sub=sub),
        grid=(bsz, seq // tm),
        in_specs=[
            pl.BlockSpec((1, tm, d), lambda b, t: (b, t, 0)),
            _resident((1, d)),
            pl.BlockSpec((1, 1, d), lambda b, t: (b, 0, 0)),
            pl.BlockSpec((1, 1, d), lambda b, t: (b, 0, 1)),
            _resident_layer(w_in_stack, index),
            _resident((1, fdim)),
            _resident((1, vdim)),
        ],
        out_specs=pl.BlockSpec((1, tm, heads * vdim), lambda b, t: (b, t, 0)),
        out_shape=jax.ShapeDtypeStruct((bsz, seq, heads * vdim), _BF16),
        scratch_shapes=[pltpu.VMEM((heads, vdim, kdim), _F32)],
        compiler_params=_cparams("parallel", "arbitrary"),
        name="hgrn2_mixer",
    )(x, gain_mix.reshape(1, d), mod, mod, w_in_stack, lb.reshape(1, fdim), g_gain.reshape(1, vdim))


def _gelu_tanh(x):
    half = 0.5 * x
    return half + half * jnp.tanh(x * (0.7978845608028654 + 0.035677408136300125 * (x * x)))


def _rglru_kernel(x_ref, gn_ref, sh_ref, sc_ref, w_ref, cw_ref, cb_ref, wa_ref, ba_ref, wi_ref, bi_ref,
                  lam_ref, o_ref, hc_ref, xh_ref, *, phases, sub):
    t = pl.program_id(1)

    @pl.when(t == 0)
    def _init():
        hc_ref[...] = jnp.zeros_like(hc_ref)
        xh_ref[...] = jnp.zeros_like(xh_ref)

    width = o_ref.shape[2]
    ng = sub // phases
    shift = ng.bit_length() - 1
    i_r = lax.broadcasted_iota(jnp.int32, (sub, sub), 0)
    i_c = lax.broadcasted_iota(jnp.int32, (sub, sub), 1)
    time_of = lambda i: jnp.bitwise_and(i, ng - 1) * phases + jnp.right_shift(i, shift)
    perm = (i_c == time_of(i_r)).astype(_BF16)
    unperm = (i_r == time_of(i_c)).astype(_BF16)
    slab = lambda val, p: val[p * ng:(p + 1) * ng, :]
    grow = lax.broadcasted_iota(jnp.int32, (ng, 1), 0)
    row = lax.broadcasted_iota(jnp.int32, (sub, 1), 0)
    taps = cw_ref.shape[0]
    nb = wa_ref.shape[0]
    bw = width // nb
    neg_lam = -lam_ref[...]
    softplus = jnp.maximum(neg_lam, 0.0) + jnp.log1p(jnp.exp(-jnp.abs(neg_lam)))
    log_a_per_r = -RG_C * softplus

    def project(rows):
        h = _norm_mod(x_ref[0, rows, :], gn_ref[...], sh_ref[0], sc_ref[0]).astype(_BF16)
        h = _dot(perm, h).astype(_BF16)
        return _dot(h, w_ref[:, :width]), _dot(h, w_ref[:, width:])

    tiles = [slice(r, r + sub) for r in range(0, x_ref.shape[1], sub)]
    z = project(tiles[0])
    for i, rows in enumerate(tiles):
        y_pre, x_br = z
        prev_group = {p: jnp.where(grow >= 1, pltpu.roll(slab(x_br, p), 1, 0), xh_ref[p:p + 1, :])
                      for p in range(phases - (taps - 1), phases)}
        for p in range(phases - (taps - 1), phases):
            xh_ref[p:p + 1, :] = slab(x_br, p)[ng - 1:ng, :]
        conv_slabs = []
        for p in range(phases):
            acc = cb_ref[...]
            for j in range(taps):
                src = p - (taps - 1) + j
                acc = acc + (slab(x_br, src) if src >= 0 else prev_group[src + phases]) * cw_ref[j:j + 1, :]
            conv_slabs.append(acc)
        x_conv = jnp.concatenate(conv_slabs, axis=0)
        if i + 1 < len(tiles):
            z = project(tiles[i + 1])

        xcb = x_conv.astype(_BF16)
        r = jnp.concatenate([_dot(xcb[:, n * bw:(n + 1) * bw], wa_ref[n]) for n in range(nb)], axis=1)
        gi = jnp.concatenate([_dot(xcb[:, n * bw:(n + 1) * bw], wi_ref[n]) for n in range(nb)], axis=1)
        r = _sigmoid_tanh(r + ba_ref[...])
        gi = _sigmoid_tanh(gi + bi_ref[...])
        log_a = r * log_a_per_r
        a = jnp.exp(log_a)
        th = jnp.tanh(log_a)
        mult = jnp.sqrt(-2.0 * th) * lax.rsqrt(1.0 - th)
        if i == 0:
            mult = jnp.where(jnp.logical_and(row == 0, t == 0), 1.0, mult)
        u = (gi * x_conv) * mult

        h_loc, a_cum = [slab(u, 0)], [slab(a, 0)]
        for p in range(1, phases):
            h_loc.append(slab(a, p) * h_loc[-1] + slab(u, p))
            a_cum.append(slab(a, p) * a_cum[-1])
        ga, gu = a_cum[-1], h_loc[-1]
        d = 1
        while d < ng:
            keep = grow >= d
            a_sh = jnp.where(keep, pltpu.roll(ga, d, 0), 1.0)
            u_sh = jnp.where(keep, pltpu.roll(gu, d, 0), 0.0)
            ga, gu = ga * a_sh, gu + ga * u_sh
            d *= 2
        carry = hc_ref[...]
        h_end = gu + ga * carry
        hc_ref[...] = h_end[ng - 1:ng, :]
        h_in = jnp.where(grow >= 1, pltpu.roll(h_end, 1, 0), carry)
        hs = jnp.concatenate([h_loc[p] + a_cum[p] * h_in for p in range(phases)], axis=0)
        out = (hs * _gelu_tanh(y_pre)).astype(_BF16)
        o_ref[0, rows, :] = _dot(unperm, out).astype(o_ref.dtype)


def _rglru_mixer(x, gain_mix, mod, w_in_stack, index, conv_w, conv_b, w_a, b_a, w_i, b_i, lam, tm):
    bsz, seq, d = x.shape
    two_w = w_in_stack.shape[2]
    width = two_w // 2
    nb, bw, _ = w_a.shape
    phases = RG_SCAN_PHASES
    sub = min(tm, RG_SUBTILE)
    ng = sub // phases
    assert tm % sub == 0 and sub % phases == 0 and ng % SUBLANES == 0 and ng & (ng - 1) == 0
    assert phases >= RG_CONV_WIDTH
    row = lambda v: v.reshape(1, width)
    return pl.pallas_call(
        functools.partial(_rglru_kernel, phases=phases, sub=sub),
        grid=(bsz, seq // tm),
        in_specs=[
            pl.BlockSpec((1, tm, d), lambda b, t: (b, t, 0)),
            _resident((1, d)),
            pl.BlockSpec((1, 1, d), lambda b, t: (b, 0, 0)),
            pl.BlockSpec((1, 1, d), lambda b, t: (b, 0, 1)),
            _resident_layer(w_in_stack, index),
            _resident((RG_CONV_WIDTH, width)),
            _resident((1, width)),
            _resident((nb, bw, bw)),
            _resident((1, width)),
            _resident((nb, bw, bw)),
            _resident((1, width)),
            _resident((1, width)),
        ],
        out_specs=pl.BlockSpec((1, tm, width), lambda b, t: (b, t, 0)),
        out_shape=jax.ShapeDtypeStruct((bsz, seq, width), _BF16),
        scratch_shapes=[
            pltpu.VMEM((1, width), _F32),
            pltpu.VMEM((phases, width), _F32),
        ],
        compiler_params=_cparams("parallel", "arbitrary"),
        name="rglru_mixer",
    )(x, gain_mix.reshape(1, d), mod, mod, w_in_stack, conv_w, row(conv_b), w_a.astype(_BF16), row(b_a),
      w_i.astype(_BF16), row(b_i), row(lam))


def _tail_kernel(x_ref, m_ref, wo_ref, g1_ref, gn_ref, sh_ref, sc_ref, g2_ref, wup_ref, wdn_ref,
                 *rest, ff_chunk, final):
    o_ref = rest[-1]
    x1 = x_ref[0] + g1_ref[0] * _dot(m_ref[0], wo_ref[...])
    h = _norm_mod(x1, gn_ref[...], sh_ref[0], sc_ref[0]).astype(_BF16)
    acc = jnp.zeros_like(x1)
    for c in range(0, wup_ref.shape[1], ff_chunk):
        u = jnp.maximum(_dot(h, wup_ref[:, c:c + ff_chunk]), 0.0)
        acc = acc + _dot((u * u).astype(_BF16), wdn_ref[c:c + ff_chunk, :])
    x2 = x1 + g2_ref[0] * acc
    if final:
        x2 = _rms_norm(x2, rest[0][...])
    o_ref[0] = x2


def _layer_tail(x, m, w_o, mod, gain_mlp, w_up, w_down, final_gain, tm):
    bsz, seq, d = x.shape
    dm = m.shape[-1]
    dff = w_up[0].shape[2]
    final = final_gain is not None
    mod_spec = lambda idx: pl.BlockSpec((1, 1, d), lambda b, t: (b, 0, idx))
    in_specs = [
        pl.BlockSpec((1, tm, d), lambda b, t: (b, t, 0)),
        pl.BlockSpec((1, tm, dm), lambda b, t: (b, t, 0)),
        _resident_layer(*w_o),
        mod_spec(2),
        _resident((1, d)),
        mod_spec(3),
        mod_spec(4),
        mod_spec(5),
        _resident_layer(*w_up),
        _resident_layer(*w_down),
    ]
    args = [x, m, w_o[0], mod, gain_mlp.reshape(1, d), mod, mod, mod, w_up[0], w_down[0]]
    if final:
        in_specs.append(_resident((1, d)))
        args.append(final_gain.reshape(1, d))
    return pl.pallas_call(
        functools.partial(_tail_kernel, ff_chunk=min(dff, 1024), final=final),
        grid=(bsz, seq // tm),
        in_specs=in_specs,
        out_specs=pl.BlockSpec((1, tm, d), lambda b, t: (b, t, 0)),
        out_shape=jax.ShapeDtypeStruct((bsz, seq, d), _F32),
        compiler_params=_cparams("parallel", "parallel"),
        name="layer_tail",
    )(*args)


def kernel(x, c, ada_w, ada_b, norm_mix, norm_mlp, mlp_up, mlp_down, moba_wqkv, moba_wo, hgrn_w_in, hgrn_lb, hgrn_norm, hgrn_wo, rg_w_in, rg_conv_w, rg_conv_b, rg_w_a, rg_b_a, rg_w_i, rg_b_i, rg_lambda, rg_wo, final_norm):
    depth = ada_w.shape[0]
    bsz, seq, d = x.shape
    tiles = _tiles(seq)
    bf = lambda w: w.astype(_BF16)
    mlp_up, mlp_down = bf(mlp_up), bf(mlp_down)
    moba_wqkv, moba_wo, hgrn_w_in, hgrn_wo, rg_w_in, rg_wo = map(bf, (moba_wqkv, moba_wo, hgrn_w_in, hgrn_wo, rg_w_in, rg_wo))

    mod_all = _ada_modulation(c, ada_w, ada_b)
    lb_all = _hgrn_lower_bounds(hgrn_lb)
    i_a = i_b = i_c = 0
    for layer in range(depth):
        mod = mod_all[layer].reshape(bsz, 1, 6 * d)
        kind = layer % N_MIXERS
        if kind == 0:
            qkv = _project(x, norm_mix[layer], mod, moba_wqkv, i_a, _BF16, tiles["proj"])
            m = _moba_attention(qkv, MOBA_HEADS, tiles["moba_heads"])
            w_o = (moba_wo, i_a)
            i_a += 1
        elif kind == 1:
            m = _hgrn_mixer(x, norm_mix[layer], mod, hgrn_w_in, i_b, lb_all[layer], hgrn_norm[i_b],
                            HGRN_HEADS, tiles["hgrn"])
            w_o = (hgrn_wo, i_b)
            i_b += 1
        else:
            m = _rglru_mixer(x, norm_mix[layer], mod, rg_w_in, i_c, rg_conv_w[i_c], rg_conv_b[i_c],
                             rg_w_a[i_c], rg_b_a[i_c], rg_w_i[i_c], rg_b_i[i_c], rg_lambda[i_c],
                             tiles["rglru"])
            w_o = (rg_wo, i_c)
            i_c += 1
        x = _layer_tail(x, m, w_o, mod, norm_mlp[layer], (mlp_up, layer), (mlp_down, layer),
                        final_norm if layer == depth - 1 else None, tiles["tail"])
    return x
```

```python
import functools

import jax
import jax.numpy as jnp
from jax import lax
from jax.experimental import pallas as pl
from jax.experimental.pallas import tpu as pltpu

_F32 = jnp.float32
_BF16 = jnp.bfloat16

NORM_EPS = 1e-6
N_MIXERS = 3
MOBA_HEADS = 8
MOBA_BLOCK = 256
MOBA_TOPK = 3
HGRN_HEADS = 8
HGRN_CHUNK = 64
HGRN_SUBTILE = 256
RG_CONV_WIDTH = 4
RG_C = 8.0
RG_SUBTILE = 256
RG_SCAN_PHASES = 16
MASK_VALUE = -1e30

V7X_VMEM_LIMIT_BYTES = 56 * 1024 * 1024
LANES = 128
SUBLANES = 8


def _tiles(seq):
    return dict(proj=min(seq, 1024), tail=min(seq, 1024), hgrn=min(seq, 1024), rglru=min(seq, 1024), moba_heads=4)


def _cparams(*sem):
    return pltpu.CompilerParams(dimension_semantics=sem, vmem_limit_bytes=V7X_VMEM_LIMIT_BYTES)


def _resident(shape):
    zeros = (0,) * len(shape)
    return pl.BlockSpec(shape, lambda *_: zeros, pipeline_mode=pl.Buffered(1))


def _resident_layer(stack, index):
    return pl.BlockSpec((None,) + stack.shape[1:], lambda *_: (index, 0, 0), pipeline_mode=pl.Buffered(1))


def _sigmoid(x):
    return 1.0 / (1.0 + jnp.exp(-x))


def _sigmoid_tanh(x):
    return 0.5 * jnp.tanh(0.5 * x) + 0.5


def _silu_tanh(x):
    half = 0.5 * x
    return half + half * jnp.tanh(half)


def _rms_norm(x, gain):
    return x * lax.rsqrt(jnp.mean(x * x, axis=-1, keepdims=True) + NORM_EPS) * gain


def _norm_mod(x, gain, shift, scale):
    return x * lax.rsqrt(jnp.mean(x * x, axis=-1, keepdims=True) + NORM_EPS) * (gain * (1.0 + scale)) + shift


def _dot(a, b):
    return jnp.dot(a, b, preferred_element_type=_F32)


def _dot_nt(a, b):
    return lax.dot_general(a, b, (((1,), (1,)), ((), ())), preferred_element_type=_F32)


def _ada_kernel(c_ref, w_ref, b_ref, o_ref):
    c = c_ref[...]
    cond = c * _sigmoid(c)
    bsz = cond.shape[0]
    w = w_ref[0]
    c_hi = cond.astype(_BF16)
    c_lo = (cond - c_hi.astype(_F32)).astype(_BF16)
    w_hi = w.astype(_BF16)
    w_lo = (w - w_hi.astype(_F32)).astype(_BF16)
    with_hi = _dot(jnp.concatenate([c_hi, c_lo], axis=0), w_hi)
    o_ref[0] = with_hi[:bsz] + with_hi[bsz:] + _dot(c_hi, w_lo) + b_ref[0]


def _ada_modulation(c, ada_w, ada_b):
    depth, d, n = ada_w.shape
    bsz = c.shape[0]
    tn = min(n, 1536)
    return pl.pallas_call(
        _ada_kernel,
        grid=(depth, n // tn),
        in_specs=[
            pl.BlockSpec((bsz, d), lambda l, j: (0, 0)),
            pl.BlockSpec((1, d, tn), lambda l, j: (l, 0, j)),
            pl.BlockSpec((1, 1, tn), lambda l, j: (l, 0, j)),
        ],
        out_specs=pl.BlockSpec((1, bsz, tn), lambda l, j: (l, 0, j)),
        out_shape=jax.ShapeDtypeStruct((depth, bsz, n), _F32),
        compiler_params=_cparams("arbitrary", "arbitrary"),
        name="ada_modulation",
    )(c, ada_w, ada_b.reshape(depth, 1, n))


def _lb_kernel(x_ref, o_ref):
    depth = x_ref.shape[0]
    rows = [x_ref[l:l + 1, :] for l in range(depth)]
    mx = functools.reduce(jnp.maximum, rows)
    es = [jnp.exp(r - mx) for r in rows]
    den = functools.reduce(lambda a, b: a + b, es)
    first = es[0] / den
    run = first
    o_ref[0:1, :] = run - first
    for l in range(1, depth):
        run = run + es[l] / den
        o_ref[l:l + 1, :] = run - first


def _hgrn_lower_bounds(hgrn_lb):
    return pl.pallas_call(
        _lb_kernel,
        out_shape=jax.ShapeDtypeStruct(hgrn_lb.shape, _F32),
        name="hgrn_lower_bounds",
    )(hgrn_lb.astype(_F32))


def _proj_kernel(x_ref, g_ref, sh_ref, sc_ref, w_ref, o_ref, *, n_chunk):
    h = _norm_mod(x_ref[0], g_ref[...], sh_ref[0], sc_ref[0]).astype(_BF16)
    for c in range(0, w_ref.shape[1], n_chunk):
        o_ref[0, :, c:c + n_chunk] = _dot(h, w_ref[:, c:c + n_chunk]).astype(o_ref.dtype)


def _project(x, gain, mod, w_stack, index, out_dtype, tm):
    bsz, seq, d = x.shape
    n = w_stack.shape[2]
    return pl.pallas_call(
        functools.partial(_proj_kernel, n_chunk=min(n, 1024)),
        grid=(bsz, seq // tm),
        in_specs=[
            pl.BlockSpec((1, tm, d), lambda b, t: (b, t, 0)),
            _resident((1, d)),
            pl.BlockSpec((1, 1, d), lambda b, t: (b, 0, 0)),
            pl.BlockSpec((1, 1, d), lambda b, t: (b, 0, 1)),
            _resident_layer(w_stack, index),
        ],
        out_specs=pl.BlockSpec((1, tm, n), lambda b, t: (b, t, 0)),
        out_shape=jax.ShapeDtypeStruct((bsz, seq, n), out_dtype),
        compiler_params=_cparams("parallel", "parallel"),
        name="project",
    )(x, gain.reshape(1, d), mod, mod, w_stack)


def _moba_kernel(q_ref, k_ref, v_ref, o_ref, *, blk, topk, scale):
    dh = LANES
    for hd in range(q_ref.shape[2] // dh):
        cols = slice(hd * dh, (hd + 1) * dh)
        _moba_head(q_ref[0, :, cols], k_ref[0, :, cols], v_ref[0, :, cols], o_ref, cols,
                   blk=blk, topk=topk, scale=scale)


def _moba_head(q, k, v, o_ref, cols, *, blk, topk, scale):
    seq, dh = k.shape
    nblk = seq // blk

    row = lax.broadcasted_iota(jnp.int32, (nblk, seq), 0)
    col = lax.broadcasted_iota(jnp.int32, (nblk, seq), 1)
    fully_past = (row + 1) * blk <= col
    kmean = jnp.sum(k.astype(_F32).reshape(nblk, blk, dh), axis=1) * (1.0 / blk)
    kmean_hi = kmean.astype(_BF16)
    kmean_lo = (kmean - kmean_hi.astype(_F32)).astype(_BF16)
    gate_parts = _dot_nt(jnp.concatenate([kmean_hi, kmean_lo], axis=0), q)
    gate_t = gate_parts[:nblk, :] + gate_parts[nblk:, :]

    bias_rows = []
    for j in range(nblk):
        g_j = gate_t[j:j + 1, :]
        beats = jnp.logical_and(
            fully_past, jnp.logical_or(gate_t > g_j, jnp.logical_and(gate_t == g_j, row < j)))
        rank = jnp.sum(beats.astype(_F32), axis=0, keepdims=True)
        masked = jnp.logical_and(fully_past[j:j + 1, :], rank >= topk)
        bias_rows.append(jnp.where(masked, MASK_VALUE, 0.0))
    bias_t = jnp.concatenate(bias_rows + [jnp.zeros((dh - nblk, seq), _F32)], axis=0)

    key_pos = lax.broadcasted_iota(jnp.int32, (seq, dh), 0)
    lane = lax.broadcasted_iota(jnp.int32, (seq, dh), 1)
    onehot = jnp.logical_and(lane * blk <= key_pos, key_pos < (lane + 1) * blk).astype(_BF16)
    k_exp2 = (k.astype(_F32) * (scale * 1.4426950408889634)).astype(_BF16)
    k_aug = jnp.concatenate([k_exp2, onehot], axis=1)
    v_aug = jnp.concatenate([v, jnp.ones((seq, dh), _BF16)], axis=1)

    qpos = lax.broadcasted_iota(jnp.int32, (blk, blk), 0)
    kpos = lax.broadcasted_iota(jnp.int32, (blk, blk), 1)
    causal = kpos <= qpos
    order = [0] + list(range(nblk - 1, 0, -1))
    logits = {}
    for i in order:
        lo, hi = i * blk, (i + 1) * blk
        if i == 0:
            bias = jnp.zeros((blk, dh), _BF16)
        else:
            bias = jnp.transpose(bias_t[:, lo:hi]).astype(_BF16)
        q_aug = jnp.concatenate([q[lo:hi, :], bias], axis=1)
        logits[i] = _dot_nt(q_aug, k_aug[:hi, :]).astype(_BF16)
    probs = {}
    for i in order:
        lo, hi = i * blk, (i + 1) * blk
        s = logits[i]
        s_own = jnp.where(causal, s[:, lo:hi], jnp.asarray(MASK_VALUE, _BF16))
        s = s_own if i == 0 else jnp.concatenate([s[:, :lo], s_own], axis=1)
        m = jnp.max(s, axis=-1, keepdims=True)
        probs[i] = jnp.exp2(s - m)
    for i in order:
        lo, hi = i * blk, (i + 1) * blk
        o = _dot(probs[i], v_aug[:hi, :])
        o_ref[0, lo:hi, cols] = (o[:, :dh] * (1.0 / o[:, dh:dh + 1])).astype(o_ref.dtype)


def _moba_attention(qkv, heads, heads_per_step):
    bsz, seq, three_hd = qkv.shape
    dh = three_hd // (3 * heads)
    blk = MOBA_BLOCK
    assert seq % blk == 0 and dh == LANES and heads % heads_per_step == 0
    nblk = seq // blk
    topk = min(MOBA_TOPK, nblk - 1)
    groups = heads // heads_per_step
    width = heads_per_step * dh
    kern = functools.partial(_moba_kernel, blk=blk, topk=topk, scale=dh ** -0.5)
    return pl.pallas_call(
        kern,
        grid=(bsz, groups),
        in_specs=[
            pl.BlockSpec((1, seq, width), lambda b, g: (b, 0, g)),
            pl.BlockSpec((1, seq, width), lambda b, g: (b, 0, groups + g)),
            pl.BlockSpec((1, seq, width), lambda b, g: (b, 0, 2 * groups + g)),
        ],
        out_specs=pl.BlockSpec((1, seq, width), lambda b, g: (b, 0, g)),
        out_shape=jax.ShapeDtypeStruct((bsz, seq, heads * dh), _BF16),
        compiler_params=_cparams("parallel", "parallel"),
        name="moba_attention",
    )(qkv, qkv, qkv)


def _split2(x):
    hi = x.astype(_BF16)
    return hi, (x - hi.astype(_F32)).astype(_BF16)


def _hgrn_kernel(x_ref, gn_ref, sh_ref, sc_ref, w_ref, lb_ref, gain_ref, o_ref, st_ref, *, heads, chunk, sub):
    @pl.when(pl.program_id(1) == 0)
    def _init():
        st_ref[...] = jnp.zeros_like(st_ref)

    fdim = lb_ref.shape[1]
    hv = heads * gain_ref.shape[1]
    tiles = [slice(r, r + sub) for r in range(0, x_ref.shape[1], sub)]

    def project(rows):
        h = _norm_mod(x_ref[0, rows, :], gn_ref[...], sh_ref[0], sc_ref[0]).astype(_BF16)
        return (_dot(h, w_ref[:, :fdim]), _dot(h, w_ref[:, fdim:2 * fdim]),
                _dot(h, w_ref[:, 2 * fdim:2 * fdim + hv]).astype(_BF16), _dot(h, w_ref[:, 2 * fdim + hv:]))

    z = project(tiles[0])
    for i, rows in enumerate(tiles):
        gates = _hgrn_gates(z[0], z[1], lb_ref[...], chunk=chunk)
        v, zg = z[2], z[3]
        if i + 1 < len(tiles):
            z = project(tiles[i + 1])
        _hgrn_heads(gates, v, zg, gain_ref[...], o_ref, rows, st_ref, heads=heads, chunk=chunk)


def _hgrn_gates(zq, zf, lb, *, chunk):
    tm, fdim = zq.shape
    nck = tm // chunk
    q = _silu_tanh(zq)
    fgate = lb + (1.0 - lb) * _sigmoid(zf)
    k = 1.0 - fgate

    shift = chunk.bit_length() - 1
    r_i = lax.broadcasted_iota(jnp.int32, (tm, tm), 0)
    c_i = lax.broadcasted_iota(jnp.int32, (tm, tm), 1)
    block_causal = jnp.logical_and(jnp.right_shift(r_i, shift) == jnp.right_shift(c_i, shift), c_i <= r_i)
    tri = block_causal.astype(_BF16)
    b = functools.reduce(lambda a, c: a + c, [_dot(tri, part) for part in _split2(jnp.log(fgate))])

    by_chunk = lambda t: t.reshape(nck, chunk, fdim)
    b3, q3, k3 = by_chunk(b), by_chunk(q), by_chunk(k)
    b_mid = b3[:, chunk // 2:chunk // 2 + 1, :]
    b_last = b3[:, chunk - 1:chunk, :]
    flat = lambda t: t.astype(_BF16).reshape(tm, fdim)
    qd = flat(q3 * jnp.exp(b3 - b_mid))
    kd = flat(k3 * jnp.exp(b_mid - b3))
    q_in = flat(q3 * jnp.exp(b3))
    k_out = flat(k3 * jnp.exp(b_last - b3))
    decay = jnp.exp(b_last)
    return qd, kd, q_in, k_out, decay, block_causal


def _hgrn_heads(gates, v, zg, gain, o_ref, rows_out, st_ref, *, heads, chunk):
    qd, kd, q_in, k_out, decay, block_causal = gates
    tm, fdim = qd.shape
    nck = tm // chunk
    kdim = fdim // heads
    vdim = gain.shape[1]
    ksl = [slice(hd * kdim, (hd + 1) * kdim) for hd in range(heads)]
    vsl = [slice(hd * vdim, (hd + 1) * vdim) for hd in range(heads)]
    crow = [slice(c * chunk, (c + 1) * chunk) for c in range(nck)]
    causal = block_causal[:chunk, :chunk]
    a_all = [[jnp.where(causal, _dot_nt(qd[rows, ks], kd[rows, ks]), 0.0).astype(_BF16) for rows in crow]
             for ks in ksl]
    v_t = [[jnp.transpose(v[rows, vs].astype(_F32)).astype(_BF16) for rows in crow] for vs in vsl]
    upd = [[_dot(v_t[hd][c], k_out[crow[c], ksl[hd]]) for c in range(nck)] for hd in range(heads)]
    for hd in range(heads):
        st = st_ref[hd]
        outs = []
        for c in range(nck):
            lhs = jnp.concatenate([q_in[crow[c], ksl[hd]], a_all[hd][c]], axis=1)
            rhs_t = jnp.concatenate([st.astype(_BF16), v_t[hd][c]], axis=1)
            outs.append(_dot_nt(lhs, rhs_t))
            st = decay[c, :, ksl[hd]] * st + upd[hd][c]
        st_ref[hd] = st
        o = jnp.concatenate(outs, axis=0)
        zg_h = zg[:, vsl[hd]]
        o_ref[0, rows_out, vsl[hd]] = (_rms_norm(o, gain) * _silu_tanh(zg_h)).astype(o_ref.dtype)


def _hgrn_mixer(x, gain_mix, mod, w_in_stack, index, lb, g_gain, heads, tm):
    bsz, seq, d = x.shape
    n = w_in_stack.shape[2]
    fdim = lb.shape[-1]
    vdim = g_gain.shape[-1]
    kdim = fdim // heads
    sub = min(tm, HGRN_SUBTILE)
    assert kdim == LANES and vdim == LANES and n == 2 * fdim + 2 * heads * vdim
    assert HGRN_CHUNK & (HGRN_CHUNK - 1) == 0 and sub % HGRN_CHUNK == 0 and tm % sub == 0
    return pl.pallas_call(
        functools.partial(_hgrn_kernel, heads=heads, chunk=HGRN_CHUNK, sub=sub),
        grid=(bsz, seq // tm),
        in_specs=[
            pl.BlockSpec((1, tm, d), lambda b, t: (b, t, 0)),
            _resident((1, d)),
            pl.BlockSpec((1, 1, d), lambda b, t: (b, 0, 0)),
            pl.BlockSpec((1, 1, d), lambda b, t: (b, 0, 1)),
            _resident_layer(w_in_stack, index),
            _resident((1, fdim)),
            _resident((1, vdim)),
        ],
        out_specs=pl.BlockSpec((1, tm, heads * vdim), lambda b, t: (b, t, 0)),
        out_shape=jax.ShapeDtypeStruct((bsz, seq, heads * vdim), _BF16),
        scratch_shapes=[pltpu.VMEM((heads, vdim, kdim), _F32)],
        compiler_params=_cparams("parallel", "arbitrary"),
        name="hgrn2_mixer",
    )(x, gain_mix.reshape(1, d), mod, mod, w_in_stack, lb.reshape(1, fdim), g_gain.reshape(1, vdim))


def _gelu_tanh(x):
    half = 0.5 * x
    return half + half * jnp.tanh(x * (0.7978845608028654 + 0.035677408136300125 * (x * x)))


def _rglru_kernel(x_ref, gn_ref, sh_ref, sc_ref, w_ref, cw_ref, cb_ref, wa_ref, ba_ref, wi_ref, bi_ref,
                  lam_ref, o_ref, hc_ref, xh_ref, *, phases, sub):
    t = pl.program_id(1)

    @pl.when(t == 0)
    def _init():
        hc_ref[...] = jnp.zeros_like(hc_ref)
        xh_ref[...] = jnp.zeros_like(xh_ref)

    width = o_ref.shape[2]
    ng = sub // phases
    shift = ng.bit_length() - 1
    i_r = lax.broadcasted_iota(jnp.int32, (sub, sub), 0)
    i_c = lax.broadcasted_iota(jnp.int32, (sub, sub), 1)
    time_of = lambda i: jnp.bitwise_and(i, ng - 1) * phases + jnp.right_shift(i, shift)
    perm = (i_c == time_of(i_r)).astype(_BF16)
    unperm = (i_r == time_of(i_c)).astype(_BF16)
    slab = lambda val, p: val[p * ng:(p + 1) * ng, :]
    grow = lax.broadcasted_iota(jnp.int32, (ng, 1), 0)
    row = lax.broadcasted_iota(jnp.int32, (sub, 1), 0)
    taps = cw_ref.shape[0]
    nb = wa_ref.shape[0]
    bw = width // nb
    neg_lam = -lam_ref[...]
    softplus = jnp.maximum(neg_lam, 0.0) + jnp.log1p(jnp.exp(-jnp.abs(neg_lam)))
    log_a_per_r = -RG_C * softplus

    def project(rows):
        h = _norm_mod(x_ref[0, rows, :], gn_ref[...], sh_ref[0], sc_ref[0]).astype(_BF16)
        h = _dot(perm, h).astype(_BF16)
        return _dot(h, w_ref[:, :width]), _dot(h, w_ref[:, width:])

    tiles = [slice(r, r + sub) for r in range(0, x_ref.shape[1], sub)]
    z = project(tiles[0])
    for i, rows in enumerate(tiles):
        y_pre, x_br = z
        prev_group = {p: jnp.where(grow >= 1, pltpu.roll(slab(x_br, p), 1, 0), xh_ref[p:p + 1, :])
                      for p in range(phases - (taps - 1), phases)}
        for p in range(phases - (taps - 1), phases):
            xh_ref[p:p + 1, :] = slab(x_br, p)[ng - 1:ng, :]
        conv_slabs = []
        for p in range(phases):
            acc = cb_ref[...]
            for j in range(taps):
                src = p - (taps - 1) + j
                acc = acc + (slab(x_br, src) if src >= 0 else prev_group[src + phases]) * cw_ref[j:j + 1, :]
            conv_slabs.append(acc)
        x_conv = jnp.concatenate(conv_slabs, axis=0)
        if i + 1 < len(tiles):
            z = project(tiles[i + 1])

        xcb = x_conv.astype(_BF16)
        r = jnp.concatenate([_dot(xcb[:, n * bw:(n + 1) * bw], wa_ref[n]) for n in range(nb)], axis=1)
        gi = jnp.concatenate([_dot(xcb[:, n * bw:(n + 1) * bw], wi_ref[n]) for n in range(nb)], axis=1)
        r = _sigmoid_tanh(r + ba_ref[...])
        gi = _sigmoid_tanh(gi + bi_ref[...])
        log_a = r * log_a_per_r
        a = jnp.exp(log_a)
        th = jnp.tanh(log_a)
        mult = jnp.sqrt(-2.0 * th) * lax.rsqrt(1.0 - th)
        if i == 0:
            mult = jnp.where(jnp.logical_and(row == 0, t == 0), 1.0, mult)
        u = (gi * x_conv) * mult

        h_loc, a_cum = [slab(u, 0)], [slab(a, 0)]
        for p in range(1, phases):
            h_loc.append(slab(a, p) * h_loc[-1] + slab(u, p))
            a_cum.append(slab(a, p) * a_cum[-1])
        ga, gu = a_cum[-1], h_loc[-1]
        d = 1
        while d < ng:
            keep = grow >= d
            a_sh = jnp.where(keep, pltpu.roll(ga, d, 0), 1.0)
            u_sh = jnp.where(keep, pltpu.roll(gu, d, 0), 0.0)
            ga, gu = ga * a_sh, gu + ga * u_sh
            d *= 2
        carry = hc_ref[...]
        h_end = gu + ga * carry
        hc_ref[...] = h_end[ng - 1:ng, :]
        h_in = jnp.where(grow >= 1, pltpu.roll(h_end, 1, 0), carry)
        hs = jnp.concatenate([h_loc[p] + a_cum[p] * h_in for p in range(phases)], axis=0)
        out = (hs * _gelu_tanh(y_pre)).astype(_BF16)
        o_ref[0, rows, :] = _dot(unperm, out).astype(o_ref.dtype)


def _rglru_mixer(x, gain_mix, mod, w_in_stack, index, conv_w, conv_b, w_a, b_a, w_i, b_i, lam, tm):
    bsz, seq, d = x.shape
    two_w = w_in_stack.shape[2]
    width = two_w // 2
    nb, bw, _ = w_a.shape
    phases = RG_SCAN_PHASES
    sub = min(tm, RG_SUBTILE)
    ng = sub // phases
    assert tm % sub == 0 and sub % phases == 0 and ng % SUBLANES == 0 and ng & (ng - 1) == 0
    assert phases >= RG_CONV_WIDTH
    row = lambda v: v.reshape(1, width)
    return pl.pallas_call(
        functools.partial(_rglru_kernel, phases=phases, sub=sub),
        grid=(bsz, seq // tm),
        in_specs=[
            pl.BlockSpec((1, tm, d), lambda b, t: (b, t, 0)),
            _resident((1, d)),
            pl.BlockSpec((1, 1, d), lambda b, t: (b, 0, 0)),
            pl.BlockSpec((1, 1, d), lambda b, t: (b, 0, 1)),
            _resident_layer(w_in_stack, index),
            _resident((RG_CONV_WIDTH, width)),
            _resident((1, width)),
            _resident((nb, bw, bw)),
            _resident((1, width)),
            _resident((nb, bw, bw)),
            _resident((1, width)),
            _resident((1, width)),
        ],
        out_specs=pl.BlockSpec((1, tm, width), lambda b, t: (b, t, 0)),
        out_shape=jax.ShapeDtypeStruct((bsz, seq, width), _BF16),
        scratch_shapes=[
            pltpu.VMEM((1, width), _F32),
            pltpu.VMEM((phases, width), _F32),
        ],
        compiler_params=_cparams("parallel", "arbitrary"),
        name="rglru_mixer",
    )(x, gain_mix.reshape(1, d), mod, mod, w_in_stack, conv_w, row(conv_b), w_a.astype(_BF16), row(b_a),
      w_i.astype(_BF16), row(b_i), row(lam))


def _tail_kernel(x_ref, m_ref, wo_ref, g1_ref, gn_ref, sh_ref, sc_ref, g2_ref, wup_ref, wdn_ref,
                 *rest, ff_chunk, final):
    o_ref = rest[-1]
    x1 = x_ref[0] + g1_ref[0] * _dot(m_ref[0], wo_ref[...])
    h = _norm_mod(x1, gn_ref[...], sh_ref[0], sc_ref[0]).astype(_BF16)
    acc = jnp.zeros_like(x1)
    for c in range(0, wup_ref.shape[1], ff_chunk):
        u = jnp.maximum(_dot(h, wup_ref[:, c:c + ff_chunk]), 0.0)
        acc = acc + _dot((u * u).astype(_BF16), wdn_ref[c:c + ff_chunk, :])
    x2 = x1 + g2_ref[0] * acc
    if final:
        x2 = _rms_norm(x2, rest[0][...])
    o_ref[0] = x2


def _layer_tail(x, m, w_o, mod, gain_mlp, w_up, w_down, final_gain, tm):
    bsz, seq, d = x.shape
    dm = m.shape[-1]
    dff = w_up[0].shape[2]
    final = final_gain is not None
    mod_spec = lambda idx: pl.BlockSpec((1, 1, d), lambda b, t: (b, 0, idx))
    in_specs = [
        pl.BlockSpec((1, tm, d), lambda b, t: (b, t, 0)),
        pl.BlockSpec((1, tm, dm), lambda b, t: (b, t, 0)),
        _resident_layer(*w_o),
        mod_spec(2),
        _resident((1, d)),
        mod_spec(3),
        mod_spec(4),
        mod_spec(5),
        _resident_layer(*w_up),
        _resident_layer(*w_down),
    ]
    args = [x, m, w_o[0], mod, gain_mlp.reshape(1, d), mod, mod, mod, w_up[0], w_down[0]]
    if final:
        in_specs.append(_resident((1, d)))
        args.append(final_gain.reshape(1, d))
    return pl.pallas_call(
        functools.partial(_tail_kernel, ff_chunk=min(dff, 1024), final=final),
        grid=(bsz, seq // tm),
        in_specs=in_specs,
        out_specs=pl.BlockSpec((1, tm, d), lambda b, t: (b, t, 0)),
        out_shape=jax.ShapeDtypeStruct((bsz, seq, d), _F32),
        compiler_params=_cparams("parallel", "parallel"),
        name="layer_tail",
    )(*args)


def kernel(x, c, ada_w, ada_b, norm_mix, norm_mlp, mlp_up, mlp_down, moba_wqkv, moba_wo, hgrn_w_in, hgrn_lb, hgrn_norm, hgrn_wo, rg_w_in, rg_conv_w, rg_conv_b, rg_w_a, rg_b_a, rg_w_i, rg_b_i, rg_lambda, rg_wo, final_norm):
    depth = ada_w.shape[0]
    bsz, seq, d = x.shape
    tiles = _tiles(seq)
    bf = lambda w: w.astype(_BF16)
    mlp_up, mlp_down = bf(mlp_up), bf(mlp_down)
    moba_wqkv, moba_wo, hgrn_w_in, hgrn_wo, rg_w_in, rg_wo = map(bf, (moba_wqkv, moba_wo, hgrn_w_in, hgrn_wo, rg_w_in, rg_wo))

    mod_all = _ada_modulation(c, ada_w, ada_b)
    lb_all = _hgrn_lower_bounds(hgrn_lb)
    i_a = i_b = i_c = 0
    for layer in range(depth):
        mod = mod_all[layer].reshape(bsz, 1, 6 * d)
        kind = layer % N_MIXERS
        if kind == 0:
            qkv = _project(x, norm_mix[layer], mod, moba_wqkv, i_a, _BF16, tiles["proj"])
            m = _moba_attention(qkv, MOBA_HEADS, tiles["moba_heads"])
            w_o = (moba_wo, i_a)
            i_a += 1
        elif kind == 1:
            m = _hgrn_mixer(x, norm_mix[layer], mod, hgrn_w_in, i_b, lb_all[layer], hgrn_norm[i_b],
                            HGRN_HEADS, tiles["hgrn"])
            w_o = (hgrn_wo, i_b)
            i_b += 1
        else:
            m = _rglru_mixer(x, norm_mix[layer], mod, rg_w_in, i_c, rg_conv_w[i_c], rg_conv_b[i_c],
                             rg_w_a[i_c], rg_b_a[i_c], rg_w_i[i_c], rg_b_i[i_c], rg_lambda[i_c],
                             tiles["rglru"])
            w_o = (rg_wo, i_c)
            i_c += 1
        x = _layer_tail(x, m, w_o, mod, norm_mlp[layer], (mlp_up, layer), (mlp_down, layer),
                        final_norm if layer == depth - 1 else None, tiles["tail"])
    return x
```

```python
import functools

import jax
import jax.numpy as jnp
from jax import lax
from jax.experimental import pallas as pl
from jax.experimental.pallas import tpu as pltpu

_F32 = jnp.float32
_BF16 = jnp.bfloat16

NORM_EPS = 1e-6
N_MIXERS = 3
MOBA_HEADS = 8
MOBA_BLOCK = 256
MOBA_TOPK = 3
HGRN_HEADS = 8
HGRN_CHUNK = 64
HGRN_SUBTILE = 256
RG_CONV_WIDTH = 4
RG_C = 8.0
RG_SUBTILE = 256
RG_SCAN_PHASES = 16
MASK_VALUE = -1e30

V7X_VMEM_LIMIT_BYTES = 56 * 1024 * 1024
LANES = 128
SUBLANES = 8


def _tiles(seq):
    return dict(proj=min(seq, 1024), tail=min(seq, 1024), hgrn=min(seq, 1024), rglru=min(seq, 1024), moba_heads=4)


def _cparams(*sem):
    return pltpu.CompilerParams(dimension_semantics=sem, vmem_limit_bytes=V7X_VMEM_LIMIT_BYTES)


def _resident(shape):
    zeros = (0,) * len(shape)
    return pl.BlockSpec(shape, lambda *_: zeros, pipeline_mode=pl.Buffered(1))


def _resident_layer(stack, index):
    return pl.BlockSpec((None,) + stack.shape[1:], lambda *_: (index, 0, 0), pipeline_mode=pl.Buffered(1))


def _sigmoid(x):
    return 1.0 / (1.0 + jnp.exp(-x))


def _sigmoid_tanh(x):
    return 0.5 * jnp.tanh(0.5 * x) + 0.5


def _silu_tanh(x):
    half = 0.5 * x
    return half + half * jnp.tanh(half)


def _rms_norm(x, gain):
    return x * lax.rsqrt(jnp.mean(x * x, axis=-1, keepdims=True) + NORM_EPS) * gain


def _norm_mod(x, gain, shift, scale):
    return x * lax.rsqrt(jnp.mean(x * x, axis=-1, keepdims=True) + NORM_EPS) * (gain * (1.0 + scale)) + shift


def _dot(a, b):
    return jnp.dot(a, b, preferred_element_type=_F32)


def _dot_nt(a, b):
    return lax.dot_general(a, b, (((1,), (1,)), ((), ())), preferred_element_type=_F32)


def _ada_kernel(c_ref, w_ref, b_ref, o_ref):
    c = c_ref[...]
    cond = c * _sigmoid(c)
    bsz = cond.shape[0]
    w = w_ref[0]
    c_hi = cond.astype(_BF16)
    c_lo = (cond - c_hi.astype(_F32)).astype(_BF16)
    w_hi = w.astype(_BF16)
    w_lo = (w - w_hi.astype(_F32)).astype(_BF16)
    with_hi = _dot(jnp.concatenate([c_hi, c_lo], axis=0), w_hi)
    o_ref[0] = with_hi[:bsz] + with_hi[bsz:] + _dot(c_hi, w_lo) + b_ref[0]


def _ada_modulation(c, ada_w, ada_b):
    depth, d, n = ada_w.shape
    bsz = c.shape[0]
    tn = min(n, 1536)
    return pl.pallas_call(
        _ada_kernel,
        grid=(depth, n // tn),
        in_specs=[
            pl.BlockSpec((bsz, d), lambda l, j: (0, 0)),
            pl.BlockSpec((1, d, tn), lambda l, j: (l, 0, j)),
            pl.BlockSpec((1, 1, tn), lambda l, j: (l, 0, j)),
        ],
        out_specs=pl.BlockSpec((1, bsz, tn), lambda l, j: (l, 0, j)),
        out_shape=jax.ShapeDtypeStruct((depth, bsz, n), _F32),
        compiler_params=_cparams("arbitrary", "arbitrary"),
        name="ada_modulation",
    )(c, ada_w, ada_b.reshape(depth, 1, n))


def _lb_kernel(x_ref, o_ref):
    depth = x_ref.shape[0]
    rows = [x_ref[l:l + 1, :] for l in range(depth)]
    mx = functools.reduce(jnp.maximum, rows)
    es = [jnp.exp(r - mx) for r in rows]
    den = functools.reduce(lambda a, b: a + b, es)
    first = es[0] / den
    run = first
    o_ref[0:1, :] = run - first
    for l in range(1, depth):
        run = run + es[l] / den
        o_ref[l:l + 1, :] = run - first


def _hgrn_lower_bounds(hgrn_lb):
    return pl.pallas_call(
        _lb_kernel,
        out_shape=jax.ShapeDtypeStruct(hgrn_lb.shape, _F32),
        name="hgrn_lower_bounds",
    )(hgrn_lb.astype(_F32))


def _proj_kernel(x_ref, g_ref, sh_ref, sc_ref, w_ref, cs_ref, o_ref, *, n_chunk):
    h = _norm_mod(x_ref[0], g_ref[...], sh_ref[0], sc_ref[0]).astype(_BF16)
    for c in range(0, w_ref.shape[1], n_chunk):
        o_ref[0, :, c:c + n_chunk] = (_dot(h, w_ref[:, c:c + n_chunk]) * cs_ref[:, c:c + n_chunk]).astype(o_ref.dtype)


def _project(x, gain, mod, w_stack, index, col_scale, out_dtype, tm):
    bsz, seq, d = x.shape
    n = w_stack.shape[2]
    return pl.pallas_call(
        functools.partial(_proj_kernel, n_chunk=min(n, 1024)),
        grid=(bsz, seq // tm),
        in_specs=[
            pl.BlockSpec((1, tm, d), lambda b, t: (b, t, 0)),
            _resident((1, d)),
            pl.BlockSpec((1, 1, d), lambda b, t: (b, 0, 0)),
            pl.BlockSpec((1, 1, d), lambda b, t: (b, 0, 1)),
            _resident_layer(w_stack, index),
            _resident((1, n)),
        ],
        out_specs=pl.BlockSpec((1, tm, n), lambda b, t: (b, t, 0)),
        out_shape=jax.ShapeDtypeStruct((bsz, seq, n), out_dtype),
        compiler_params=_cparams("parallel", "parallel"),
        name="project",
    )(x, gain.reshape(1, d), mod, mod, w_stack, col_scale.reshape(1, n))


def _moba_key_scale(scale):
    return scale * 1.4426950408889634


def _moba_kernel(q_ref, k_ref, v_ref, o_ref, *, blk, topk, scale):
    dh = LANES
    for hd in range(q_ref.shape[2] // dh):
        cols = slice(hd * dh, (hd + 1) * dh)
        _moba_head(q_ref[0, :, cols], k_ref[0, :, cols], v_ref[0, :, cols], o_ref, cols,
                   blk=blk, topk=topk, scale=scale)


def _moba_head(q, k, v, o_ref, cols, *, blk, topk, scale):
    seq, dh = k.shape
    nblk = seq // blk
    key_scale = _moba_key_scale(scale)

    row = lax.broadcasted_iota(jnp.int32, (nblk, seq), 0)
    col = lax.broadcasted_iota(jnp.int32, (nblk, seq), 1)
    fully_past = (row + 1) * blk <= col
    kmean = jnp.sum(k.astype(_F32).reshape(nblk, blk, dh), axis=1) * (1.0 / (blk * key_scale))
    kmean_hi = kmean.astype(_BF16)
    kmean_lo = (kmean - kmean_hi.astype(_F32)).astype(_BF16)
    gate_parts = _dot_nt(jnp.concatenate([kmean_hi, kmean_lo], axis=0), q)
    gate_t = gate_parts[:nblk, :] + gate_parts[nblk:, :]

    bias_rows = []
    for j in range(nblk):
        g_j = gate_t[j:j + 1, :]
        beats = jnp.logical_and(
            fully_past, jnp.logical_or(gate_t > g_j, jnp.logical_and(gate_t == g_j, row < j)))
        rank = jnp.sum(beats.astype(_F32), axis=0, keepdims=True)
        masked = jnp.logical_and(fully_past[j:j + 1, :], rank >= topk)
        bias_rows.append(jnp.where(masked, MASK_VALUE, 0.0))
    bias_t = jnp.concatenate(bias_rows + [jnp.zeros((dh - nblk, seq), _F32)], axis=0)

    key_pos = lax.broadcasted_iota(jnp.int32, (seq, dh), 0)
    lane = lax.broadcasted_iota(jnp.int32, (seq, dh), 1)
    onehot = jnp.logical_and(lane * blk <= key_pos, key_pos < (lane + 1) * blk).astype(_BF16)
    k_aug = jnp.concatenate([k, onehot], axis=1)
    v_aug = jnp.concatenate([v, jnp.ones((seq, dh), _BF16)], axis=1)

    qpos = lax.broadcasted_iota(jnp.int32, (blk, blk), 0)
    kpos = lax.broadcasted_iota(jnp.int32, (blk, blk), 1)
    causal = kpos <= qpos
    order = [0] + list(range(nblk - 1, 0, -1))
    logits = {}
    for i in order:
        lo, hi = i * blk, (i + 1) * blk
        if i == 0:
            bias = jnp.zeros((blk, dh), _BF16)
        else:
            bias = jnp.transpose(bias_t[:, lo:hi]).astype(_BF16)
        q_aug = jnp.concatenate([q[lo:hi, :], bias], axis=1)
        logits[i] = _dot_nt(q_aug, k_aug[:hi, :]).astype(_BF16)
    probs = {}
    for i in order:
        lo, hi = i * blk, (i + 1) * blk
        s = logits[i]
        s_own = jnp.where(causal, s[:, lo:hi], jnp.asarray(MASK_VALUE, _BF16))
        s = s_own if i == 0 else jnp.concatenate([s[:, :lo], s_own], axis=1)
        m = jnp.max(s, axis=-1, keepdims=True)
        probs[i] = jnp.exp2(s - m)
    for i in order:
        lo, hi = i * blk, (i + 1) * blk
        o = _dot(probs[i], v_aug[:hi, :])
        o_ref[0, lo:hi, cols] = (o[:, :dh] * (1.0 / o[:, dh:dh + 1])).astype(o_ref.dtype)


def _moba_attention(qkv, heads, heads_per_step):
    bsz, seq, three_hd = qkv.shape
    dh = three_hd // (3 * heads)
    blk = MOBA_BLOCK
    assert seq % blk == 0 and dh == LANES and heads % heads_per_step == 0
    nblk = seq // blk
    topk = min(MOBA_TOPK, nblk - 1)
    groups = heads // heads_per_step
    width = heads_per_step * dh
    kern = functools.partial(_moba_kernel, blk=blk, topk=topk, scale=dh ** -0.5)
    return pl.pallas_call(
        kern,
        grid=(bsz, groups),
        in_specs=[
            pl.BlockSpec((1, seq, width), lambda b, g: (b, 0, g)),
            pl.BlockSpec((1, seq, width), lambda b, g: (b, 0, groups + g)),
            pl.BlockSpec((1, seq, width), lambda b, g: (b, 0, 2 * groups + g)),
        ],
        out_specs=pl.BlockSpec((1, seq, width), lambda b, g: (b, 0, g)),
        out_shape=jax.ShapeDtypeStruct((bsz, seq, heads * dh), _BF16),
        compiler_params=_cparams("parallel", "parallel"),
        name="moba_attention",
    )(qkv, qkv, qkv)


def _split2(x):
    hi = x.astype(_BF16)
    return hi, (x - hi.astype(_F32)).astype(_BF16)


def _hgrn_kernel(x_ref, gn_ref, sh_ref, sc_ref, w_ref, lb_ref, gain_ref, o_ref, st_ref, *, heads, chunk, sub):
    @pl.when(pl.program_id(1) == 0)
    def _init():
        st_ref[...] = jnp.zeros_like(st_ref)

    fdim = lb_ref.shape[1]
    hv = heads * gain_ref.shape[1]
    tiles = [slice(r, r + sub) for r in range(0, x_ref.shape[1], sub)]

    def project(rows):
        h = _norm_mod(x_ref[0, rows, :], gn_ref[...], sh_ref[0], sc_ref[0]).astype(_BF16)
        return (_dot(h, w_ref[:, :fdim]), _dot(h, w_ref[:, fdim:2 * fdim]),
                _dot(h, w_ref[:, 2 * fdim:2 * fdim + hv]).astype(_BF16), _dot(h, w_ref[:, 2 * fdim + hv:]))

    z = project(tiles[0])
    for i, rows in enumerate(tiles):
        gates = _hgrn_gates(z[0], z[1], lb_ref[...], chunk=chunk)
        v, zg = z[2], z[3]
        if i + 1 < len(tiles):
            z = project(tiles[i + 1])
        _hgrn_heads(gates, v, zg, gain_ref[...], o_ref, rows, st_ref, heads=heads, chunk=chunk)


def _hgrn_gates(zq, zf, lb, *, chunk):
    tm, fdim = zq.shape
    nck = tm // chunk
    q = _silu_tanh(zq)
    fgate = lb + (1.0 - lb) * _sigmoid(zf)
    k = 1.0 - fgate

    shift = chunk.bit_length() - 1
    r_i = lax.broadcasted_iota(jnp.int32, (tm, tm), 0)
    c_i = lax.broadcasted_iota(jnp.int32, (tm, tm), 1)
    block_causal = jnp.logical_and(jnp.right_shift(r_i, shift) == jnp.right_shift(c_i, shift), c_i <= r_i)
    tri = block_causal.astype(_BF16)
    b = functools.reduce(lambda a, c: a + c, [_dot(tri, part) for part in _split2(jnp.log(fgate))])

    by_chunk = lambda t: t.reshape(nck, chunk, fdim)
    b3, q3, k3 = by_chunk(b), by_chunk(q), by_chunk(k)
    b_mid = b3[:, chunk // 2:chunk // 2 + 1, :]
    b_last = b3[:, chunk - 1:chunk, :]
    flat = lambda t: t.astype(_BF16).reshape(tm, fdim)
    qd = flat(q3 * jnp.exp(b3 - b_mid))
    kd = flat(k3 * jnp.exp(b_mid - b3))
    q_in = flat(q3 * jnp.exp(b3))
    k_out = flat(k3 * jnp.exp(b_last - b3))
    decay = jnp.exp(b_last)
    return qd, kd, q_in, k_out, decay, block_causal


def _hgrn_heads(gates, v, zg, gain, o_ref, rows_out, st_ref, *, heads, chunk):
    qd, kd, q_in, k_out, decay, block_causal = gates
    tm, fdim = qd.shape
    nck = tm // chunk
    kdim = fdim // heads
    vdim = gain.shape[1]
    ksl = [slice(hd * kdim, (hd + 1) * kdim) for hd in range(heads)]
    vsl = [slice(hd * vdim, (hd + 1) * vdim) for hd in range(heads)]
    crow = [slice(c * chunk, (c + 1) * chunk) for c in range(nck)]
    causal = block_causal[:chunk, :chunk]
    a_all = [[jnp.where(causal, _dot_nt(qd[rows, ks], kd[rows, ks]), 0.0).astype(_BF16) for rows in crow]
             for ks in ksl]
    v_t = [[jnp.transpose(v[rows, vs].astype(_F32)).astype(_BF16) for rows in crow] for vs in vsl]
    upd = [[_dot(v_t[hd][c], k_out[crow[c], ksl[hd]]) for c in range(nck)] for hd in range(heads)]
    for hd in range(heads):
        st = st_ref[hd]
        outs = []
        for c in range(nck):
            lhs = jnp.concatenate([q_in[crow[c], ksl[hd]], a_all[hd][c]], axis=1)
            rhs_t = jnp.concatenate([st.astype(_BF16), v_t[hd][c]], axis=1)
            outs.append(_dot_nt(lhs, rhs_t))
            st = decay[c, :, ksl[hd]] * st + upd[hd][c]
        st_ref[hd] = st
        o = jnp.concatenate(outs, axis=0)
        zg_h = zg[:, vsl[hd]]
        o_ref[0, rows_out, vsl[hd]] = (_rms_norm(o, gain) * _silu_tanh(zg_h)).astype(o_ref.dtype)


def _hgrn_mixer(x, gain_mix, mod, w_in_stack, index, lb, g_gain, heads, tm):
    bsz, seq, d = x.shape
    n = w_in_stack.shape[2]
    fdim = lb.shape[-1]
    vdim = g_gain.shape[-1]
    kdim = fdim // heads
    sub = min(tm, HGRN_SUBTILE)
    assert kdim == LANES and vdim == LANES and n == 2 * fdim + 2 * heads * vdim
    assert HGRN_CHUNK & (HGRN_CHUNK - 1) == 0 and sub % HGRN_CHUNK == 0 and tm % sub == 0
    return pl.pallas_call(
        functools.partial(_hgrn_kernel, heads=heads, chunk=HGRN_CHUNK, sub=sub),
        grid=(bsz, seq // tm),
        in_specs=[
            pl.BlockSpec((1, tm, d), lambda b, t: (b, t, 0)),
            _resident((1, d)),
            pl.BlockSpec((1, 1, d), lambda b, t: (b, 0, 0)),
            pl.BlockSpec((1, 1, d), lambda b, t: (b, 0, 1)),
            _resident_layer(w_in_stack, index),
            _resident((1, fdim)),
            _resident((1, vdim)),
        ],
        out_specs=pl.BlockSpec((1, tm, heads * vdim), lambda b, t: (b, t, 0)),
        out_shape=jax.ShapeDtypeStruct((bsz, seq, heads * vdim), _BF16),
        scratch_shapes=[pltpu.VMEM((heads, vdim, kdim), _F32)],
        compiler_params=_cparams("parallel", "arbitrary"),
        name="hgrn2_mixer",
    )(x, gain_mix.reshape(1, d), mod, mod, w_in_stack, lb.reshape(1, fdim), g_gain.reshape(1, vdim))


def _gelu_tanh(x):
    half = 0.5 * x
    return half + half * jnp.tanh(x * (0.7978845608028654 + 0.035677408136300125 * (x * x)))


def _rglru_kernel(x_ref, gn_ref, sh_ref, sc_ref, w_ref, cw_ref, cb_ref, wa_ref, ba_ref, wi_ref, bi_ref,
                  lam_ref, o_ref, hc_ref, xh_ref, *, phases, sub):
    t = pl.program_id(1)

    @pl.when(t == 0)
    def _init():
        hc_ref[...] = jnp.zeros_like(hc_ref)
        xh_ref[...] = jnp.zeros_like(xh_ref)

    width = o_ref.shape[2]
    ng = sub // phases
    shift = ng.bit_length() - 1
    i_r = lax.broadcasted_iota(jnp.int32, (sub, sub), 0)
    i_c = lax.broadcasted_iota(jnp.int32, (sub, sub), 1)
    time_of = lambda i: jnp.bitwise_and(i, ng - 1) * phases + jnp.right_shift(i, shift)
    perm = (i_c == time_of(i_r)).astype(_BF16)
    unperm = (i_r == time_of(i_c)).astype(_BF16)
    slab = lambda val, p: val[p * ng:(p + 1) * ng, :]
    grow = lax.broadcasted_iota(jnp.int32, (ng, 1), 0)
    row = lax.broadcasted_iota(jnp.int32, (sub, 1), 0)
    taps = cw_ref.shape[0]
    nb = wa_ref.shape[0]
    bw = width // nb
    neg_lam = -lam_ref[...]
    softplus = jnp.maximum(neg_lam, 0.0) + jnp.log1p(jnp.exp(-jnp.abs(neg_lam)))
    log_a_per_r = -RG_C * softplus

    def project(rows):
        h = _norm_mod(x_ref[0, rows, :], gn_ref[...], sh_ref[0], sc_ref[0]).astype(_BF16)
        h = _dot(perm, h).astype(_BF16)
        return _dot(h, w_ref[:, :width]), _dot(h, w_ref[:, width:])

    tiles = [slice(r, r + sub) for r in range(0, x_ref.shape[1], sub)]
    z = project(tiles[0])
    for i, rows in enumerate(tiles):
        y_pre, x_br = z
        prev_group = {p: jnp.where(grow >= 1, pltpu.roll(slab(x_br, p), 1, 0), xh_ref[p:p + 1, :])
                      for p in range(phases - (taps - 1), phases)}
        for p in range(phases - (taps - 1), phases):
            xh_ref[p:p + 1, :] = slab(x_br, p)[ng - 1:ng, :]
        conv_slabs = []
        for p in range(phases):
            acc = cb_ref[...]
            for j in range(taps):
                src = p - (taps - 1) + j
                acc = acc + (slab(x_br, src) if src >= 0 else prev_group[src + phases]) * cw_ref[j:j + 1, :]
            conv_slabs.append(acc)
        x_conv = jnp.concatenate(conv_slabs, axis=0)
        if i + 1 < len(tiles):
            z = project(tiles[i + 1])

        xcb = x_conv.astype(_BF16)
        r = jnp.concatenate([_dot(xcb[:, n * bw:(n + 1) * bw], wa_ref[n]) for n in range(nb)], axis=1)
        gi = jnp.concatenate([_dot(xcb[:, n * bw:(n + 1) * bw], wi_ref[n]) for n in range(nb)], axis=1)
        r = _sigmoid_tanh(r + ba_ref[...])
        gi = _sigmoid_tanh(gi + bi_ref[...])
        log_a = r * log_a_per_r
        a = jnp.exp(log_a)
        th = jnp.tanh(log_a)
        mult = jnp.sqrt(-2.0 * th) * lax.rsqrt(1.0 - th)
        if i == 0:
            mult = jnp.where(jnp.logical_and(row == 0, t == 0), 1.0, mult)
        u = (gi * x_conv) * mult

        h_loc, a_cum = [slab(u, 0)], [slab(a, 0)]
        for p in range(1, phases):
            h_loc.append(slab(a, p) * h_loc[-1] + slab(u, p))
            a_cum.append(slab(a, p) * a_cum[-1])
        ga, gu = a_cum[-1], h_loc[-1]
        d = 1
        while d < ng:
            keep = grow >= d
            a_sh = jnp.where(keep, pltpu.roll(ga, d, 0), 1.0)
            u_sh = jnp.where(keep, pltpu.roll(gu, d, 0), 0.0)
            ga, gu = ga * a_sh, gu + ga * u_sh
            d *= 2
        carry = hc_ref[...]
        h_end = gu + ga * carry
        hc_ref[...] = h_end[ng - 1:ng, :]
        h_in = jnp.where(grow >= 1, pltpu.roll(h_end, 1, 0), carry)
        hs = jnp.concatenate([h_loc[p] + a_cum[p] * h_in for p in range(phases)], axis=0)
        out = (hs * _gelu_tanh(y_pre)).astype(_BF16)
        o_ref[0, rows, :] = _dot(unperm, out).astype(o_ref.dtype)


def _rglru_mixer(x, gain_mix, mod, w_in_stack, index, conv_w, conv_b, w_a, b_a, w_i, b_i, lam, tm):
    bsz, seq, d = x.shape
    two_w = w_in_stack.shape[2]
    width = two_w // 2
    nb, bw, _ = w_a.shape
    phases = RG_SCAN_PHASES
    sub = min(tm, RG_SUBTILE)
    ng = sub // phases
    assert tm % sub == 0 and sub % phases == 0 and ng % SUBLANES == 0 and ng & (ng - 1) == 0
    assert phases >= RG_CONV_WIDTH
    row = lambda v: v.reshape(1, width)
    return pl.pallas_call(
        functools.partial(_rglru_kernel, phases=phases, sub=sub),
        grid=(bsz, seq // tm),
        in_specs=[
            pl.BlockSpec((1, tm, d), lambda b, t: (b, t, 0)),
            _resident((1, d)),
            pl.BlockSpec((1, 1, d), lambda b, t: (b, 0, 0)),
            pl.BlockSpec((1, 1, d), lambda b, t: (b, 0, 1)),
            _resident_layer(w_in_stack, index),
            _resident((RG_CONV_WIDTH, width)),
            _resident((1, width)),
            _resident((nb, bw, bw)),
            _resident((1, width)),
            _resident((nb, bw, bw)),
            _resident((1, width)),
            _resident((1, width)),
        ],
        out_specs=pl.BlockSpec((1, tm, width), lambda b, t: (b, t, 0)),
        out_shape=jax.ShapeDtypeStruct((bsz, seq, width), _BF16),
        scratch_shapes=[
            pltpu.VMEM((1, width), _F32),
            pltpu.VMEM((phases, width), _F32),
        ],
        compiler_params=_cparams("parallel", "arbitrary"),
        name="rglru_mixer",
    )(x, gain_mix.reshape(1, d), mod, mod, w_in_stack, conv_w, row(conv_b), w_a.astype(_BF16), row(b_a),
      w_i.astype(_BF16), row(b_i), row(lam))


def _tail_kernel(x_ref, m_ref, wo_ref, g1_ref, gn_ref, sh_ref, sc_ref, g2_ref, wup_ref, wdn_ref,
                 *rest, ff_chunk, final):
    o_ref = rest[-1]
    x1 = x_ref[0] + g1_ref[0] * _dot(m_ref[0], wo_ref[...])
    h = _norm_mod(x1, gn_ref[...], sh_ref[0], sc_ref[0]).astype(_BF16)
    acc = jnp.zeros_like(x1)
    for c in range(0, wup_ref.shape[1], ff_chunk):
        u = jnp.maximum(_dot(h, wup_ref[:, c:c + ff_chunk]), 0.0)
        acc = acc + _dot((u * u).astype(_BF16), wdn_ref[c:c + ff_chunk, :])
    x2 = x1 + g2_ref[0] * acc
    if final:
        x2 = _rms_norm(x2, rest[0][...])
    o_ref[0] = x2


def _layer_tail(x, m, w_o, mod, gain_mlp, w_up, w_down, final_gain, tm):
    bsz, seq, d = x.shape
    dm = m.shape[-1]
    dff = w_up[0].shape[2]
    final = final_gain is not None
    mod_spec = lambda idx: pl.BlockSpec((1, 1, d), lambda b, t: (b, 0, idx))
    in_specs = [
        pl.BlockSpec((1, tm, d), lambda b, t: (b, t, 0)),
        pl.BlockSpec((1, tm, dm), lambda b, t: (b, t, 0)),
        _resident_layer(*w_o),
        mod_spec(2),
        _resident((1, d)),
        mod_spec(3),
        mod_spec(4),
        mod_spec(5),
        _resident_layer(*w_up),
        _resident_layer(*w_down),
    ]
    args = [x, m, w_o[0], mod, gain_mlp.reshape(1, d), mod, mod, mod, w_up[0], w_down[0]]
    if final:
        in_specs.append(_resident((1, d)))
        args.append(final_gain.reshape(1, d))
    return pl.pallas_call(
        functools.partial(_tail_kernel, ff_chunk=min(dff, 1024), final=final),
        grid=(bsz, seq // tm),
        in_specs=in_specs,
        out_specs=pl.BlockSpec((1, tm, d), lambda b, t: (b, t, 0)),
        out_shape=jax.ShapeDtypeStruct((bsz, seq, d), _F32),
        compiler_params=_cparams("parallel", "parallel"),
        name="layer_tail",
    )(*args)


def kernel(x, c, ada_w, ada_b, norm_mix, norm_mlp, mlp_up, mlp_down, moba_wqkv, moba_wo, hgrn_w_in, hgrn_lb, hgrn_norm, hgrn_wo, rg_w_in, rg_conv_w, rg_conv_b, rg_w_a, rg_b_a, rg_w_i, rg_b_i, rg_lambda, rg_wo, final_norm):
    depth = ada_w.shape[0]
    bsz, seq, d = x.shape
    tiles = _tiles(seq)
    bf = lambda w: w.astype(_BF16)
    mlp_up, mlp_down = bf(mlp_up), bf(mlp_down)
    moba_wqkv, moba_wo, hgrn_w_in, hgrn_wo, rg_w_in, rg_wo = map(bf, (moba_wqkv, moba_wo, hgrn_w_in, hgrn_wo, rg_w_in, rg_wo))

    mod_all = _ada_modulation(c, ada_w, ada_b)
    lb_all = _hgrn_lower_bounds(hgrn_lb)
    i_a = i_b = i_c = 0
    for layer in range(depth):
        mod = mod_all[layer].reshape(bsz, 1, 6 * d)
        kind = layer % N_MIXERS
        if kind == 0:
            hd_all = moba_wqkv.shape[2] // 3
            key_scale = _moba_key_scale((hd_all // MOBA_HEADS) ** -0.5)
            col_scale = jnp.concatenate([jnp.ones((hd_all,), _F32), jnp.full((hd_all,), key_scale, _F32),
                                         jnp.ones((hd_all,), _F32)])
            qkv = _project(x, norm_mix[layer], mod, moba_wqkv, i_a, col_scale, _BF16, tiles["proj"])
            m = _moba_attention(qkv, MOBA_HEADS, tiles["moba_heads"])
            w_o = (moba_wo, i_a)
            i_a += 1
        elif kind == 1:
            m = _hgrn_mixer(x, norm_mix[layer], mod, hgrn_w_in, i_b, lb_all[layer], hgrn_norm[i_b],
                            HGRN_HEADS, tiles["hgrn"])
            w_o = (hgrn_wo, i_b)
            i_b += 1
        else:
            m = _rglru_mixer(x, norm_mix[layer], mod, rg_w_in, i_c, rg_conv_w[i_c], rg_conv_b[i_c],
                             rg_w_a[i_c], rg_b_a[i_c], rg_w_i[i_c], rg_b_i[i_c], rg_lambda[i_c],
                             tiles["rglru"])
            w_o = (rg_wo, i_c)
            i_c += 1
        x = _layer_tail(x, m, w_o, mod, norm_mlp[layer], (mlp_up, layer), (mlp_down, layer),
                        final_norm if layer == depth - 1 else None, tiles["tail"])
    return x
```
